```python
import math
import jax, jax.numpy as jnp
from jax import lax
import numpy as np

D_MODEL = 1024
BATCH = 16
SEQ = 2048
DEPTH = 1

HEAD_DIM = 64
BLOCK = 128
N_META = 16
META_PAD = BLOCK - N_META
WINDOW = 128
ROPE_THETA = 10000.0
RMS_EPS = 1e-6
NEG_INF = -1e30

SWA_Q_HEADS = 16
SWA_KV_HEADS = 4
SWA_GROUP = SWA_Q_HEADS // SWA_KV_HEADS
SWA_WIDTH = SWA_Q_HEADS * HEAD_DIM

DIFF_HEADS = 8
DIFF_V_DIM = 2 * HEAD_DIM
DIFF_WIDTH = DIFF_HEADS * DIFF_V_DIM

D_FF = ((8 * D_MODEL // 3 + 255) // 256) * 256

QA_COLS = SWA_Q_HEADS * HEAD_DIM
KA_COLS = SWA_KV_HEADS * HEAD_DIM
VA_COLS = SWA_KV_HEADS * HEAD_DIM
QB_COLS = DIFF_HEADS * 2 * HEAD_DIM
KB_COLS = DIFF_HEADS * 2 * HEAD_DIM
VB_COLS = DIFF_WIDTH
GATE_COLS = 2 * D_MODEL
IN_COLS = QA_COLS + KA_COLS + VA_COLS + QB_COLS + KB_COLS + VB_COLS + GATE_COLS
SPLITS = (
    QA_COLS,
    QA_COLS + KA_COLS,
    QA_COLS + KA_COLS + VA_COLS,
    QA_COLS + KA_COLS + VA_COLS + QB_COLS,
    QA_COLS + KA_COLS + VA_COLS + QB_COLS + KB_COLS,
    QA_COLS + KA_COLS + VA_COLS + QB_COLS + KB_COLS + VB_COLS,
)

kernel_name = "hybrid_swa_sink_diffattn_gated_encoder"


def rms_norm(x, g):
    xf = x.astype(jnp.float32)
    y = xf * lax.rsqrt(jnp.mean(xf * xf, axis=-1, keepdims=True) + RMS_EPS)
    return (y * g.astype(jnp.float32)).astype(x.dtype)


def rope(x, pos):
    d = x.shape[-1]
    half = d // 2
    inv_freq = ROPE_THETA ** (-jnp.arange(0, d, 2, dtype=jnp.float32) / d)
    ang = pos.astype(jnp.float32)[:, None] * inv_freq[None, :]
    shape = (1, x.shape[1]) + (1,) * (x.ndim - 3) + (half,)
    cos = jnp.cos(ang).reshape(shape)
    sin = jnp.sin(ang).reshape(shape)
    xf = x.astype(jnp.float32)
    x1, x2 = xf[..., :half], xf[..., half:]
    return jnp.concatenate([x1 * cos - x2 * sin, x2 * cos + x1 * sin], axis=-1).astype(x.dtype)


def windowed_gqa_sink_attention(q, k, v, sink, pos, is_real):
    B, Lp = q.shape[0], q.shape[1]
    nb = Lp // BLOCK
    scale = HEAD_DIM ** -0.5
    k_meta = k[:, META_PAD:BLOCK]
    v_meta = v[:, META_PAD:BLOCK]
    padw = ((0, 0), (BLOCK, BLOCK), (0, 0), (0, 0))
    k_p = jnp.pad(k, padw)
    v_p = jnp.pad(v, padw)
    pos_p = jnp.pad(pos, (BLOCK, BLOCK))
    real_p = jnp.pad(is_real, (BLOCK, BLOCK))
    qb = q.reshape(B, nb, BLOCK, SWA_KV_HEADS, SWA_GROUP, HEAD_DIM).transpose(1, 0, 2, 3, 4, 5)
    sink_g = sink.astype(jnp.float32).reshape(SWA_KV_HEADS, SWA_GROUP)

    def one_block(args):
        i, qi = args
        start = i * BLOCK
        kb = lax.dynamic_slice_in_dim(k_p, start, 3 * BLOCK, axis=1)
        vb = lax.dynamic_slice_in_dim(v_p, start, 3 * BLOCK, axis=1)
        kpos = lax.dynamic_slice_in_dim(pos_p, start, 3 * BLOCK)
        kreal = lax.dynamic_slice_in_dim(real_p, start, 3 * BLOCK)
        qpos = lax.dynamic_slice_in_dim(pos, start, BLOCK)
        mask = kreal[None, :] & (jnp.abs(qpos[:, None] - kpos[None, :]) <= WINDOW)
        s_band = jnp.einsum('bqhgd,bkhd->bhgqk', qi, kb).astype(jnp.float32) * scale
        s_band = jnp.where(mask, s_band, NEG_INF)
        s_meta = jnp.einsum('bqhgd,bmhd->bhgqm', qi, k_meta).astype(jnp.float32) * scale
        s_sink = jnp.broadcast_to(sink_g[None, :, :, None, None], s_meta.shape[:-1] + (1,))
        p = jax.nn.softmax(jnp.concatenate([s_band, s_meta, s_sink], axis=-1), axis=-1)
        p_band = p[..., :3 * BLOCK].astype(v.dtype)
        p_meta = p[..., 3 * BLOCK:3 * BLOCK + N_META].astype(v.dtype)
        return (jnp.einsum('bhgqk,bkhd->bqhgd', p_band, vb)
                + jnp.einsum('bhgqm,bmhd->bqhgd', p_meta, v_meta))

    o = lax.map(one_block, (jnp.arange(nb), qb))
    return o.transpose(1, 0, 2, 3, 4, 5).reshape(B, Lp, SWA_WIDTH)


def differential_attention(q, k, v, lam, lambda_init, subln_gain, is_key):
    B, Lp = q.shape[0], q.shape[1]
    nb = Lp // BLOCK
    scale = HEAD_DIM ** -0.5
    key_bias = jnp.where(is_key, 0.0, NEG_INF).astype(jnp.float32)
    qb = q.reshape(B, nb, BLOCK, DIFF_HEADS, 2, HEAD_DIM).transpose(1, 0, 2, 3, 4, 5)

    def one_block(qi):
        s = jnp.einsum('bqhcd,bkhcd->bhcqk', qi, k).astype(jnp.float32) * scale + key_bias
        p = jax.nn.softmax(s, axis=-1)
        a = (p[:, :, 0] - lam * p[:, :, 1]).astype(v.dtype)
        return jnp.einsum('bhqk,bkhe->bqhe', a, v)

    o = lax.map(one_block, qb)
    o = o.transpose(1, 0, 2, 3, 4).reshape(B, Lp, DIFF_HEADS, DIFF_V_DIM)
    o = rms_norm(o, subln_gain) * (1.0 - lambda_init)
    return o.reshape(B, Lp, DIFF_WIDTH).astype(v.dtype)


def setup_inputs(seed: int = 0) -> dict:
    key = jax.random.key(seed)
    ks = jax.random.split(key, 20)
    f32 = jnp.float32

    def nrm(k, shape, scale):
        return jax.random.normal(k, shape, f32) * scale

    def gain(k, shape):
        return 1.0 + 0.02 * jax.random.normal(k, shape, f32)

    return {
        "x": nrm(ks[0], (BATCH, SEQ, D_MODEL), 1.0),
        "meta_tokens": nrm(ks[1], (N_META, D_MODEL), 1.0),
        "pre_mix_gain": gain(ks[2], (DEPTH, D_MODEL)),
        "w_in": nrm(ks[3], (DEPTH, D_MODEL, IN_COLS), D_MODEL ** -0.5),
        "b_gate": nrm(ks[4], (DEPTH, GATE_COLS), 0.1),
        "attn_sink": nrm(ks[5], (DEPTH, SWA_Q_HEADS), 0.5),
        "lambda_q1": nrm(ks[6], (DEPTH, HEAD_DIM), 0.1),
        "lambda_k1": nrm(ks[7], (DEPTH, HEAD_DIM), 0.1),
        "lambda_q2": nrm(ks[8], (DEPTH, HEAD_DIM), 0.1),
        "lambda_k2": nrm(ks[9], (DEPTH, HEAD_DIM), 0.1),
        "diff_subln_gain": gain(ks[10], (DEPTH, DIFF_V_DIM)),
        "w_branch_swa": nrm(ks[11], (DEPTH, SWA_WIDTH, D_MODEL), SWA_WIDTH ** -0.5),
        "w_branch_diff": nrm(ks[12], (DEPTH, DIFF_WIDTH, D_MODEL), DIFF_WIDTH ** -0.5),
        "w_out": nrm(ks[13], (DEPTH, D_MODEL, D_MODEL), D_MODEL ** -0.5),
        "post_mix_gain": gain(ks[14], (DEPTH, D_MODEL)),
        "pre_ffn_gain": gain(ks[15], (DEPTH, D_MODEL)),
        "w_ffn_in": nrm(ks[16], (DEPTH, D_MODEL, 2 * D_FF), D_MODEL ** -0.5),
        "w_ffn_out": nrm(ks[17], (DEPTH, D_FF, D_MODEL), D_FF ** -0.5),
        "post_ffn_gain": gain(ks[18], (DEPTH, D_MODEL)),
    }


def reference(x, meta_tokens, pre_mix_gain, w_in, b_gate, attn_sink, lambda_q1, lambda_k1,
              lambda_q2, lambda_k2, diff_subln_gain, w_branch_swa, w_branch_diff, w_out,
              post_mix_gain, pre_ffn_gain, w_ffn_in, w_ffn_out, post_ffn_gain):
    B = x.shape[0]
    filler = jnp.zeros((B, META_PAD, D_MODEL), x.dtype)
    meta = jnp.broadcast_to(meta_tokens[None].astype(x.dtype), (B, N_META, D_MODEL))
    h = jnp.concatenate([filler, meta, x], axis=1)
    Lp = h.shape[1]
    pos = jnp.arange(Lp, dtype=jnp.int32) - META_PAD
    is_real = pos >= N_META
    is_key = pos >= 0

    for l in range(DEPTH):
        lambda_init = 0.8 - 0.6 * math.exp(-0.3 * l)
        u = rms_norm(h, pre_mix_gain[l])
        proj = u @ w_in[l]
        qa, ka, va, qb, kb, vb, g = jnp.split(proj, SPLITS, axis=-1)
        qa = rope(qa.reshape(B, Lp, SWA_Q_HEADS, HEAD_DIM), pos)
        ka = rope(ka.reshape(B, Lp, SWA_KV_HEADS, HEAD_DIM), pos)
        va = va.reshape(B, Lp, SWA_KV_HEADS, HEAD_DIM)
        qb = rope(qb.reshape(B, Lp, DIFF_HEADS, 2, HEAD_DIM), pos)
        kb = rope(kb.reshape(B, Lp, DIFF_HEADS, 2, HEAD_DIM), pos)
        vb = vb.reshape(B, Lp, DIFF_HEADS, DIFF_V_DIM)

        o_swa = windowed_gqa_sink_attention(qa, ka, va, attn_sink[l], pos, is_real)
        lam = (jnp.exp(jnp.sum(lambda_q1[l].astype(jnp.float32) * lambda_k1[l].astype(jnp.float32)))
               - jnp.exp(jnp.sum(lambda_q2[l].astype(jnp.float32) * lambda_k2[l].astype(jnp.float32)))
               + lambda_init)
        o_diff = differential_attention(qb, kb, vb, lam, lambda_init, diff_subln_gain[l], is_key)

        gates = jax.nn.sigmoid(g + b_gate[l])
        g_swa, g_diff = gates[..., :D_MODEL], gates[..., D_MODEL:]
        merged = g_swa * (o_swa @ w_branch_swa[l]) + g_diff * (o_diff @ w_branch_diff[l])
        h = h + rms_norm(merged @ w_out[l], post_mix_gain[l])

        u = rms_norm(h, pre_ffn_gain[l])
        gate_up = u @ w_ffn_in[l]
        f = (jax.nn.silu(gate_up[..., :D_FF]) * gate_up[..., D_FF:]) @ w_ffn_out[l]
        h = h + rms_norm(f, post_ffn_gain[l])

    return h[:, BLOCK:]
```

```python
import functools
import math

import jax
import jax.numpy as jnp
from jax import lax
from jax.experimental import pallas as pl
from jax.experimental.pallas import tpu as pltpu

F32 = jnp.float32
BF16 = jnp.bfloat16

D_MODEL = 1024
SEQ = 2048
HEAD_DIM = 64
HALF = HEAD_DIM // 2
BLOCK = 128
N_META = 16
META_PAD = BLOCK - N_META
WINDOW = 128
ROPE_THETA = 10000.0
RMS_EPS = 1e-6
NEG_INF = -1e30
SWA_Q_HEADS = 16
SWA_KV_HEADS = 4
DIFF_HEADS = 8
D_FF = 2816
LANES = 128

QA_OFF, QB_OFF, KB_OFF, VB_OFF, G_OFF, KA_OFF, VA_OFF = 0, 1024, 2048, 3072, 4096, 6144, 6400
IN_COLS = 6656
PROJ_TN = 512
BAND = 3 * BLOCK
SWA_KEYS = BAND + BLOCK

VMEM_LIMIT = 56 * 1024 * 1024


def _cparams(n_axes):
    return pltpu.CompilerParams(dimension_semantics=("arbitrary",) * n_axes, vmem_limit_bytes=VMEM_LIMIT)


def _nt_dot(a, b):
    return lax.dot_general(a, b, (((1,), (1,)), ((), ())), preferred_element_type=F32)


def _inproj_kernel(x_ref, g_ref, w_ref, b_ref, cos_ref, sa_ref, sb_ref, o_ref, u_ref, *, tm, rc):
    j = pl.program_id(1)
    nrc = tm // rc

    @pl.when(j == 0)
    def _():
        def body(c, carry):
            r0 = pl.multiple_of(c * rc, rc)
            xx = x_ref[pl.ds(r0, rc), :]
            ms = jnp.mean(xx * xx, axis=-1, keepdims=True)
            u_ref[pl.ds(r0, rc), :] = ((xx * lax.rsqrt(ms + RMS_EPS)) * g_ref[...]).astype(BF16)
            return carry
        lax.fori_loop(0, nrc, body, 0)

    def rope(t, r0, scale):
        out = (t * cos_ref[pl.ds(r0, rc), :]
               + pltpu.roll(t, LANES - HALF, 1) * sa_ref[pl.ds(r0, rc), :]
               + pltpu.roll(t, HALF, 1) * sb_ref[pl.ds(r0, rc), :])
        return out * scale if scale != 1.0 else out

    def run(kinds):
        def body(c, carry):
            r0 = pl.multiple_of(c * rc, rc)
            acc = jnp.dot(u_ref[pl.ds(r0, rc), :], w_ref[...], preferred_element_type=F32)
            for ci, kind in enumerate(kinds):
                t = acc[:, ci * LANES:(ci + 1) * LANES]
                if kind == "ropeq":
                    t = rope(t, r0, HEAD_DIM ** -0.5)
                elif kind == "ropek":
                    t = rope(t, r0, 1.0)
                elif kind == "gate":
                    t = jax.nn.sigmoid(t + b_ref[:, ci * LANES:(ci + 1) * LANES])
                o_ref[pl.ds(r0, rc), ci * LANES:(ci + 1) * LANES] = t.astype(BF16)
            return carry
        lax.fori_loop(0, nrc, body, 0)

    pl.when(j < 4)(lambda: run(["ropeq"] * 4))
    pl.when((j == 4) | (j == 5))(lambda: run(["ropek"] * 4))
    pl.when((j == 6) | (j == 7))(lambda: run(["plain"] * 4))
    pl.when((j >= 8) & (j < 12))(lambda: run(["gate"] * 4))
    pl.when(j == 12)(lambda: run(["ropek", "ropek", "plain", "plain"]))


def _inproj(x2d, gain, w, bias, cos, sa, sb, *, tm, rc):
    n = x2d.shape[0]
    kern = functools.partial(_inproj_kernel, tm=tm, rc=rc)
    return pl.pallas_call(
        kern,
        grid=(n // tm, IN_COLS // PROJ_TN),
        in_specs=[
            pl.BlockSpec((tm, D_MODEL), lambda i, j: (i, 0)),
            pl.BlockSpec((1, D_MODEL), lambda i, j: (0, 0)),
            pl.BlockSpec((D_MODEL, PROJ_TN), lambda i, j: (0, j)),
            pl.BlockSpec((1, PROJ_TN), lambda i, j: (0, j)),
            pl.BlockSpec((tm, LANES), lambda i, j: (0, 0)),
            pl.BlockSpec((tm, LANES), lambda i, j: (0, 0)),
            pl.BlockSpec((tm, LANES), lambda i, j: (0, 0)),
        ],
        out_specs=pl.BlockSpec((tm, PROJ_TN), lambda i, j: (i, j)),
        out_shape=jax.ShapeDtypeStruct((n, IN_COLS), BF16),
        scratch_shapes=[pltpu.VMEM((tm, D_MODEL), BF16)],
        compiler_params=_cparams(2),
        name="inproj",
    )(x2d, gain, w, bias, cos, sa, sb)


def _rope_tables(pos):
    inv_freq = ROPE_THETA ** (-jnp.arange(0, HEAD_DIM, 2, dtype=F32) / HEAD_DIM)
    ang = pos.astype(F32)[:, None] * inv_freq[None, :]
    lane = jnp.arange(LANES)
    cos = jnp.cos(ang)[:, lane % HALF]
    sin = jnp.sin(ang)[:, lane % HALF]
    first = (lane % HEAD_DIM) < HALF
    return cos, jnp.where(first, -sin, 0.0), jnp.where(first, 0.0, sin)


def _swa_kernel(sink_ref, q_ref, k_ref, v_ref, km_ref, vm_ref, o_ref, kx_ref, vx_ref, kmx_ref, vmx_ref):
    lo = lax.broadcasted_iota(jnp.int32, (1, LANES), 1) < HEAD_DIM

    def expand(src, dst, sl):
        for lb in range(2):
            blk = src[sl, lb * LANES:(lb + 1) * LANES].astype(F32)
            rolled = pltpu.roll(blk, HEAD_DIM, 1)
            g0, g1 = 2 * lb, 2 * lb + 1
            dst[2 * g0, sl, :] = jnp.where(lo, blk, 0.0).astype(BF16)
            dst[2 * g0 + 1, sl, :] = jnp.where(lo, 0.0, rolled).astype(BF16)
            dst[2 * g1, sl, :] = jnp.where(lo, rolled, 0.0).astype(BF16)
            dst[2 * g1 + 1, sl, :] = jnp.where(lo, 0.0, blk).astype(BF16)

    def expand_body(c, carry):
        sl = pl.ds(pl.multiple_of(c * 256, 256), 256)
        expand(k_ref, kx_ref, sl)
        expand(v_ref, vx_ref, sl)
        return carry
    lax.fori_loop(0, SEQ // 256, expand_body, 0)
    expand(km_ref, kmx_ref, slice(None))
    expand(vm_ref, vmx_ref, slice(None))

    row = lax.broadcasted_iota(jnp.int32, (BLOCK, SWA_KEYS), 0)
    col = lax.broadcasted_iota(jnp.int32, (BLOCK, SWA_KEYS), 1)

    def body(r, carry):
        q0 = pl.multiple_of(r * BLOCK, BLOCK)
        start = pl.multiple_of(jnp.clip(r * BLOCK - BLOCK, 0, SEQ - BAND), BLOCK)
        in_band = (col < BAND) & (jnp.abs((q0 + row) - (start + col)) <= WINDOW)
        bias = jnp.where(in_band | (col >= BAND + META_PAD), 0.0, NEG_INF).astype(F32)
        for g in range(SWA_KV_HEADS):
            kcat = [jnp.concatenate([kx_ref[2 * g + v, pl.ds(start, BAND), :], kmx_ref[2 * g + v]], axis=0)
                    for v in range(2)]
            vcat = [jnp.concatenate([vx_ref[2 * g + v, pl.ds(start, BAND), :], vmx_ref[2 * g + v]], axis=0)
                    for v in range(2)]
            for lbq in (2 * g, 2 * g + 1):
                q = q_ref[pl.ds(q0, BLOCK), lbq * LANES:(lbq + 1) * LANES]
                acc = None
                for v in range(2):
                    sink = sink_ref[0, 2 * lbq + v]
                    s = _nt_dot(q, kcat[v]) + bias
                    m = jnp.maximum(jnp.max(s, axis=-1, keepdims=True), sink)
                    p = jnp.exp(s - m)
                    l = jnp.sum(p, axis=-1, keepdims=True) + jnp.exp(sink - m)
                    pn = (p * (1.0 / l)).astype(BF16)
                    o = jnp.dot(pn, vcat[v], preferred_element_type=F32)
                    acc = o if acc is None else acc + o
                o_ref[pl.ds(q0, BLOCK), lbq * LANES:(lbq + 1) * LANES] = acc.astype(BF16)
        return carry
    lax.fori_loop(0, SEQ // BLOCK, body, 0)


def _swa(sink, proj3, proj_meta):
    b = proj3.shape[0]
    return pl.pallas_call(
        _swa_kernel,
        grid=(b,),
        in_specs=[
            pl.BlockSpec(memory_space=pltpu.SMEM),
            pl.BlockSpec((None, SEQ, 1024), lambda i: (i, 0, QA_OFF // 1024)),
            pl.BlockSpec((None, SEQ, 256), lambda i: (i, 0, KA_OFF // 256)),
            pl.BlockSpec((None, SEQ, 256), lambda i: (i, 0, VA_OFF // 256)),
            pl.BlockSpec((BLOCK, 256), lambda i: (0, KA_OFF // 256)),
            pl.BlockSpec((BLOCK, 256), lambda i: (0, VA_OFF // 256)),
        ],
        out_specs=pl.BlockSpec((None, SEQ, 1024), lambda i: (i, 0, 0)),
        out_shape=jax.ShapeDtypeStruct((b, SEQ, 1024), BF16),
        scratch_shapes=[
            pltpu.VMEM((2 * SWA_KV_HEADS, SEQ, LANES), BF16),
            pltpu.VMEM((2 * SWA_KV_HEADS, SEQ, LANES), BF16),
            pltpu.VMEM((2 * SWA_KV_HEADS, BLOCK, LANES), BF16),
            pltpu.VMEM((2 * SWA_KV_HEADS, BLOCK, LANES), BF16),
        ],
        compiler_params=_cparams(1),
        name="swa_attn",
    )(sink, proj3, proj3, proj3, proj_meta, proj_meta)


def _diff_kernel(lamv_ref, gain_ref, q_ref, k_ref, v_ref, km_ref, vm_ref, o_ref, *, lambda_init):
    lane = lax.broadcasted_iota(jnp.int32, (1, LANES), 1)
    lo = lane < HEAD_DIM
    lv = lamv_ref[...]
    lam = (jnp.exp(jnp.sum(lv[0:1] * lv[1:2], axis=-1, keepdims=True))
           - jnp.exp(jnp.sum(lv[2:3] * lv[3:4], axis=-1, keepdims=True)) + lambda_init)
    meta_bias = jnp.where(lane >= META_PAD, 0.0, NEG_INF).astype(F32)

    q = q_ref[...]
    k = k_ref[...]
    km = km_ref[...]
    zero = jnp.zeros((), BF16)

    def softmax_parts(sel):
        s = _nt_dot(q, jnp.where(sel, k, zero))
        sm = _nt_dot(q, jnp.where(sel, km, zero)) + meta_bias
        m = jnp.maximum(jnp.max(s, axis=-1, keepdims=True), jnp.max(sm, axis=-1, keepdims=True))
        e = jnp.exp(s - m)
        em = jnp.exp(sm - m)
        l = jnp.sum(e, axis=-1, keepdims=True) + jnp.sum(em, axis=-1, keepdims=True)
        return e, em, 1.0 / l

    e1, e1m, r1 = softmax_parts(lo)
    e2, e2m, r2 = softmax_parts(jnp.logical_not(lo))
    r2 = r2 * lam
    a = (e1 * r1 - e2 * r2).astype(BF16)
    am = (e1m * r1 - e2m * r2).astype(BF16)
    o = (jnp.dot(a, v_ref[...], preferred_element_type=F32)
         + jnp.dot(am, vm_ref[...], preferred_element_type=F32))
    y = o * lax.rsqrt(jnp.mean(o * o, axis=-1, keepdims=True) + RMS_EPS) * gain_ref[...]
    o_ref[...] = (y * (1.0 - lambda_init)).astype(BF16)


def _diff(lamv, gain, proj3, proj_meta, *, lambda_init, tq):
    b = proj3.shape[0]
    qb, kb, vb = QB_OFF // LANES, KB_OFF // LANES, VB_OFF // LANES
    kern = functools.partial(_diff_kernel, lambda_init=lambda_init)
    return pl.pallas_call(
        kern,
        grid=(b, DIFF_HEADS, SEQ // tq),
        in_specs=[
            pl.BlockSpec((4, HEAD_DIM), lambda i, h, t: (0, 0)),
            pl.BlockSpec((1, LANES), lambda i, h, t: (0, 0)),
            pl.BlockSpec((None, tq, LANES), lambda i, h, t: (i, t, qb + h)),
            pl.BlockSpec((None, SEQ, LANES), lambda i, h, t: (i, 0, kb + h)),
            pl.BlockSpec((None, SEQ, LANES), lambda i, h, t: (i, 0, vb + h)),
            pl.BlockSpec((BLOCK, LANES), lambda i, h, t: (0, kb + h)),
            pl.BlockSpec((BLOCK, LANES), lambda i, h, t: (0, vb + h)),
        ],
        out_specs=pl.BlockSpec((None, tq, LANES), lambda i, h, t: (i, t, h)),
        out_shape=jax.ShapeDtypeStruct((b, SEQ, DIFF_HEADS * LANES), BF16),
        compiler_params=_cparams(3),
        name="diff_attn",
    )(lamv, gain, proj3, proj3, proj3, proj_meta, proj_meta)


def _merge_kernel(x_ref, os_ref, od_ref, gs_ref, gd_ref, ws_ref, wd_ref, wo_ref, gain_ref, o_ref):
    a = jnp.dot(os_ref[...], ws_ref[...], preferred_element_type=F32)
    b = jnp.dot(od_ref[...], wd_ref[...], preferred_element_type=F32)
    merged = gs_ref[...].astype(F32) * a + gd_ref[...].astype(F32) * b
    y = jnp.dot(merged.astype(BF16), wo_ref[...], preferred_element_type=F32)
    o_ref[...] = x_ref[...] + (y * lax.rsqrt(jnp.mean(y * y, axis=-1, keepdims=True) + RMS_EPS)) * gain_ref[...]


def _merge(x2d, o_swa, o_diff, proj, ws, wd, wo, gain, *, tm):
    n = x2d.shape[0]
    const = lambda i: (0, 0)
    wspec = pl.BlockSpec((D_MODEL, D_MODEL), const, pipeline_mode=pl.Buffered(1))
    return pl.pallas_call(
        _merge_kernel,
        grid=(n // tm,),
        in_specs=[
            pl.BlockSpec((tm, D_MODEL), lambda i: (i, 0)),
            pl.BlockSpec((tm, D_MODEL), lambda i: (i, 0)),
            pl.BlockSpec((tm, D_MODEL), lambda i: (i, 0)),
            pl.BlockSpec((tm, D_MODEL), lambda i: (i, G_OFF // D_MODEL)),
            pl.BlockSpec((tm, D_MODEL), lambda i: (i, G_OFF // D_MODEL + 1)),
            wspec, wspec, wspec,
            pl.BlockSpec((1, D_MODEL), const),
        ],
        out_specs=pl.BlockSpec((tm, D_MODEL), lambda i: (i, 0)),
        out_shape=jax.ShapeDtypeStruct((n, D_MODEL), F32),
        compiler_params=_cparams(1),
        name="merge_out",
    )(x2d, o_swa, o_diff, proj, proj, ws, wd, wo, gain)


FF_CHUNK = 256


def _ffn_kernel(h_ref, g1_ref, wi_ref, wo_ref, g2_ref, o_ref, f_ref):
    h = h_ref[...]
    u = ((h * lax.rsqrt(jnp.mean(h * h, axis=-1, keepdims=True) + RMS_EPS)) * g1_ref[...]).astype(BF16)
    for c in range(D_FF // FF_CHUNK):
        gate = jnp.dot(u, wi_ref[:, c * FF_CHUNK:(c + 1) * FF_CHUNK], preferred_element_type=F32)
        up = jnp.dot(u, wi_ref[:, D_FF + c * FF_CHUNK:D_FF + (c + 1) * FF_CHUNK], preferred_element_type=F32)
        f_ref[:, c * FF_CHUNK:(c + 1) * FF_CHUNK] = (jax.nn.silu(gate) * up).astype(BF16)
    f = jnp.dot(f_ref[...], wo_ref[...], preferred_element_type=F32)
    o_ref[...] = h + (f * lax.rsqrt(jnp.mean(f * f, axis=-1, keepdims=True) + RMS_EPS)) * g2_ref[...]


def _ffn(h2d, g1, wi, wo, g2, *, tm):
    n = h2d.shape[0]
    const = lambda i: (0, 0)
    return pl.pallas_call(
        _ffn_kernel,
        grid=(n // tm,),
        in_specs=[
            pl.BlockSpec((tm, D_MODEL), lambda i: (i, 0)),
            pl.BlockSpec((1, D_MODEL), const),
            pl.BlockSpec((D_MODEL, 2 * D_FF), const, pipeline_mode=pl.Buffered(1)),
            pl.BlockSpec((D_FF, D_MODEL), const, pipeline_mode=pl.Buffered(1)),
            pl.BlockSpec((1, D_MODEL), const),
        ],
        out_specs=pl.BlockSpec((tm, D_MODEL), lambda i: (i, 0)),
        out_shape=jax.ShapeDtypeStruct((n, D_MODEL), F32),
        scratch_shapes=[pltpu.VMEM((tm, D_FF), BF16)],
        compiler_params=_cparams(1),
        name="ffn",
    )(h2d, g1, wi, wo, g2)


def _permute_cols(a):
    qa, ka, va, rest = a[..., :1024], a[..., 1024:1280], a[..., 1280:1536], a[..., 1536:]
    return jnp.concatenate([qa, rest, ka, va], axis=-1)


def kernel(x, meta_tokens, pre_mix_gain, w_in, b_gate, attn_sink, lambda_q1, lambda_k1, lambda_q2, lambda_k2,
           diff_subln_gain, w_branch_swa, w_branch_diff, w_out, post_mix_gain, pre_ffn_gain, w_ffn_in, w_ffn_out,
           post_ffn_gain):
    bsz = x.shape[0]
    depth = w_in.shape[0]
    h2d = x.reshape(bsz * SEQ, D_MODEL)
    lead = jnp.concatenate([jnp.zeros((META_PAD, D_MODEL), x.dtype), meta_tokens.astype(x.dtype)], axis=0)
    real_tables = _rope_tables(jnp.arange(SEQ, dtype=jnp.int32) + N_META)
    lead_tables = _rope_tables(jnp.arange(BLOCK, dtype=jnp.int32) - META_PAD)

    for l in range(depth):
        lambda_init = 0.8 - 0.6 * math.exp(-0.3 * l)
        w = _permute_cols(w_in[l]).astype(BF16)
        bias = jnp.zeros((1, IN_COLS), F32).at[0, G_OFF:G_OFF + 2 * D_MODEL].set(b_gate[l].astype(F32))
        gain = pre_mix_gain[l].reshape(1, D_MODEL).astype(F32)

        proj = _inproj(h2d, gain, w, bias, *real_tables, tm=SEQ, rc=256)
        if l == 0:
            proj_lead = _inproj(lead, gain, w, bias, *lead_tables, tm=BLOCK, rc=BLOCK)
        else:
            raise NotImplementedError("DEPTH > 1 is not supported")
        proj3 = proj.reshape(bsz, SEQ, IN_COLS)

        o_swa = _swa(attn_sink[l].reshape(1, SWA_Q_HEADS).astype(F32), proj3, proj_lead)
        lamv = jnp.stack([lambda_q1[l], lambda_k1[l], lambda_q2[l], lambda_k2[l]]).astype(F32)
        o_diff = _diff(lamv, diff_subln_gain[l].reshape(1, LANES).astype(F32), proj3, proj_lead,
                       lambda_init=lambda_init, tq=256)

        h2d = _merge(h2d, o_swa.reshape(bsz * SEQ, D_MODEL), o_diff.reshape(bsz * SEQ, D_MODEL), proj,
                     w_branch_swa[l].astype(BF16), w_branch_diff[l].astype(BF16), w_out[l].astype(BF16),
                     post_mix_gain[l].reshape(1, D_MODEL).astype(F32), tm=512)
        h2d = _ffn(h2d, pre_ffn_gain[l].reshape(1, D_MODEL).astype(F32), w_ffn_in[l].astype(BF16),
                   w_ffn_out[l].astype(BF16), post_ffn_gain[l].reshape(1, D_MODEL).astype(F32), tm=512)

    return h2d.reshape(bsz, SEQ, D_MODEL)
```

```python
import functools
import math

import jax
import jax.numpy as jnp
from jax import lax
from jax.experimental import pallas as pl
from jax.experimental.pallas import tpu as pltpu

F32 = jnp.float32
BF16 = jnp.bfloat16

D_MODEL = 1024
SEQ = 2048
HEAD_DIM = 64
HALF = HEAD_DIM // 2
BLOCK = 128
N_META = 16
META_PAD = BLOCK - N_META
WINDOW = 128
ROPE_THETA = 10000.0
RMS_EPS = 1e-6
NEG_INF = -1e30
SWA_Q_HEADS = 16
SWA_KV_HEADS = 4
DIFF_HEADS = 8
D_FF = 2816
LANES = 128
BF16_ROWS = 16

QA_OFF, QB_OFF, KB_OFF, VB_OFF, G_OFF, KA_OFF, VA_OFF = 0, 1024, 2048, 3072, 4096, 6144, 6400
IN_COLS = 6656
PROJ_GROUP = 512
BAND = 3 * BLOCK
KEYS_EXT = SEQ + BLOCK
KEYS_VALID = SEQ + N_META

LOG2E = math.log2(math.e)
Q_SCALE = HEAD_DIM ** -0.5 * LOG2E

VMEM_LIMIT = 56 * 1024 * 1024


def _cparams(n_axes):
    return pltpu.CompilerParams(dimension_semantics=("arbitrary",) * n_axes, vmem_limit_bytes=VMEM_LIMIT)


def _nt_dot(a, b):
    return lax.dot_general(a, b, (((1,), (1,)), ((), ())), preferred_element_type=F32)


def _resident(shape):
    return pl.BlockSpec(shape, lambda *_: (0,) * len(shape), pipeline_mode=pl.Buffered(1))


def _col_kind(col):
    if col < KB_OFF:
        return "rope", Q_SCALE
    if col < VB_OFF:
        return "rope", 1.0
    if col < G_OFF:
        return "plain", 1.0
    if col < KA_OFF:
        return "gate", 1.0
    if col < VA_OFF:
        return "rope", 1.0
    return "plain", 1.0


def _inproj_kernel(x_ref, g_ref, w_ref, b_ref, cos_ref, sa_ref, sb_ref, o_ref):
    xx = x_ref[...]
    ms = jnp.mean(xx * xx, axis=-1, keepdims=True)
    u = ((xx * lax.rsqrt(ms + RMS_EPS)) * g_ref[...]).astype(BF16)
    cos, sa, sb = cos_ref[...], sa_ref[...], sb_ref[...]
    for grp in range(IN_COLS // PROJ_GROUP):
        c0 = grp * PROJ_GROUP
        acc = jnp.dot(u, w_ref[:, c0:c0 + PROJ_GROUP], preferred_element_type=F32)
        for ci in range(PROJ_GROUP // LANES):
            col = c0 + ci * LANES
            t = acc[:, ci * LANES:(ci + 1) * LANES]
            kind, scale = _col_kind(col)
            if kind == "rope":
                t = t * cos + pltpu.roll(t, LANES - HALF, 1) * sa + pltpu.roll(t, HALF, 1) * sb
                if scale != 1.0:
                    t = t * scale
            elif kind == "gate":
                t = jax.nn.sigmoid(t + b_ref[:, col - G_OFF:col - G_OFF + LANES])
            o_ref[:, col:col + LANES] = t.astype(BF16)


def _inproj(x2d, gain, w, b_gate, cos, sa, sb, *, tm):
    n = x2d.shape[0]
    nt = cos.shape[0] // tm
    table = pl.BlockSpec((tm, LANES), lambda i: (i % nt, 0))
    return pl.pallas_call(
        _inproj_kernel,
        grid=(n // tm,),
        in_specs=[
            pl.BlockSpec((tm, D_MODEL), lambda i: (i, 0)),
            _resident((1, D_MODEL)),
            _resident((D_MODEL, IN_COLS)),
            _resident((1, 2 * D_MODEL)),
            table, table, table,
        ],
        out_specs=pl.BlockSpec((tm, IN_COLS), lambda i: (i, 0)),
        out_shape=jax.ShapeDtypeStruct((n, IN_COLS), BF16),
        compiler_params=_cparams(1),
        name="inproj",
    )(x2d, gain, w, b_gate, cos, sa, sb)


def _rope_tables(pos):
    inv_freq = ROPE_THETA ** (-jnp.arange(0, HEAD_DIM, 2, dtype=F32) / HEAD_DIM)
    ang = pos.astype(F32)[:, None] * inv_freq[None, :]
    lane = jnp.arange(LANES)
    cos = jnp.cos(ang)[:, lane % HALF]
    sin = jnp.sin(ang)[:, lane % HALF]
    first = (lane % HEAD_DIM) < HALF
    return cos, jnp.where(first, -sin, 0.0), jnp.where(first, 0.0, sin)


SWA_VROWS = HEAD_DIM + BF16_ROWS
SWA_KROWS = BAND + N_META


def _swa_kernel(sink_ref, q_ref, k_ref, v_ref, km_ref, vm_ref, o_ref, kx_ref, kmx_ref, vt_ref):
    lo = lax.broadcasted_iota(jnp.int32, (1, LANES), 1) < HEAD_DIM

    def expand_k(src, dst, sl_src, sl_dst):
        for lb in range(2):
            blk = src[sl_src, lb * LANES:(lb + 1) * LANES].astype(F32)
            rolled = pltpu.roll(blk, HEAD_DIM, 1)
            g0, g1 = 2 * lb, 2 * lb + 1
            dst[2 * g0, sl_dst, :] = jnp.where(lo, blk, 0.0).astype(BF16)
            dst[2 * g0 + 1, sl_dst, :] = jnp.where(lo, 0.0, rolled).astype(BF16)
            dst[2 * g1, sl_dst, :] = jnp.where(lo, rolled, 0.0).astype(BF16)
            dst[2 * g1 + 1, sl_dst, :] = jnp.where(lo, 0.0, blk).astype(BF16)

    def put_vt(src, sl_src, sl_dst):
        vt = src[sl_src, :].astype(F32).T
        for g in range(SWA_KV_HEADS):
            vt_ref[g, 0:HEAD_DIM, sl_dst] = vt[g * HEAD_DIM:(g + 1) * HEAD_DIM].astype(BF16)

    def prep_body(c, carry):
        sl = pl.ds(pl.multiple_of(c * 256, 256), 256)
        expand_k(k_ref, kx_ref, sl, sl)
        put_vt(v_ref, sl, sl)
        return carry
    lax.fori_loop(0, SEQ // 256, prep_body, 0)
    expand_k(km_ref, kmx_ref, slice(META_PAD, BLOCK), slice(None))
    put_vt(vm_ref, slice(None), slice(SEQ, KEYS_EXT))
    vt_ref[:, HEAD_DIM:SWA_VROWS, :] = jnp.ones((SWA_KV_HEADS, BF16_ROWS, KEYS_EXT), BF16)

    krow = lax.broadcasted_iota(jnp.int32, (SWA_KROWS, 2 * LANES), 0)
    qcol = lax.broadcasted_iota(jnp.int32, (SWA_KROWS, 2 * LANES), 1) % LANES
    first_blk = lax.broadcasted_iota(jnp.int32, (1, 2 * LANES), 1) < LANES

    def body(r, carry):
        q0 = pl.multiple_of(r * BLOCK, BLOCK)
        start = pl.multiple_of(jnp.clip(r * BLOCK - BLOCK, 0, SEQ - BAND), BLOCK)
        visible = (krow >= BAND) | (jnp.abs((start + krow) - (q0 + qcol)) <= WINDOW)
        bias = jnp.where(visible, 0.0, NEG_INF).astype(F32)
        filler_rows = jnp.zeros((META_PAD, 2 * LANES), BF16)

        chains = [(g, v) for g in range(SWA_KV_HEADS) for v in range(2)]

        def scores(g, v):
            lb0, lb1 = 2 * g, 2 * g + 1
            qq = jnp.concatenate([q_ref[pl.ds(q0, BLOCK), lb0 * LANES:(lb0 + 1) * LANES],
                                  q_ref[pl.ds(q0, BLOCK), lb1 * LANES:(lb1 + 1) * LANES]], axis=0)
            kcat = jnp.concatenate([kx_ref[2 * g + v, pl.ds(start, BAND), :], kmx_ref[2 * g + v]], axis=0)
            sink = jnp.where(first_blk, sink_ref[0, 2 * lb0 + v], sink_ref[0, 2 * lb1 + v]) * LOG2E
            s = _nt_dot(kcat, qq) + bias
            m = jnp.maximum(jnp.max(s, axis=0, keepdims=True), sink)
            return s, m, sink

        def probs(s, m, sink):
            p = jnp.exp2(s - m).astype(BF16)
            return jnp.concatenate([p[0:BAND], filler_rows, p[BAND:SWA_KROWS]], axis=0), jnp.exp2(sink - m)

        def values(g, pt, psink):
            vcat = jnp.concatenate([vt_ref[g, :, pl.ds(start, BAND)], vt_ref[g, :, SEQ:KEYS_EXT]], axis=1)
            o = jnp.dot(vcat, pt, preferred_element_type=F32)
            return o[0:HEAD_DIM] * (1.0 / (o[HEAD_DIM:HEAD_DIM + 1] + psink))

        st_a, st_b, st_c = {}, {}, {}
        for step in range(len(chains) + 2):
            if step < len(chains):
                st_a[step] = scores(*chains[step])
            if 0 <= step - 1 < len(chains):
                st_b[step - 1] = probs(*st_a.pop(step - 1))
            if 0 <= step - 2 < len(chains):
                g, v = chains[step - 2]
                st_c[(g, v)] = values(g, *st_b.pop(step - 2))
                if v == 1:
                    ot = jnp.concatenate([st_c.pop((g, 0)), st_c.pop((g, 1))], axis=0)
                    for half, lb in enumerate((2 * g, 2 * g + 1)):
                        o_ref[pl.ds(q0, BLOCK), lb * LANES:(lb + 1) * LANES] = (
                            ot[:, half * LANES:(half + 1) * LANES].T.astype(BF16))
        return carry
    lax.fori_loop(0, SEQ // BLOCK, body, 0)


def _swa(sink, proj3, proj_lead):
    b = proj3.shape[0]
    return pl.pallas_call(
        _swa_kernel,
        grid=(b,),
        in_specs=[
            pl.BlockSpec(memory_space=pltpu.SMEM),
            pl.BlockSpec((None, SEQ, 1024), lambda i: (i, 0, QA_OFF // 1024)),
            pl.BlockSpec((None, SEQ, 256), lambda i: (i, 0, KA_OFF // 256)),
            pl.BlockSpec((None, SEQ, 256), lambda i: (i, 0, VA_OFF // 256)),
            pl.BlockSpec((BLOCK, 256), lambda i: (0, KA_OFF // 256)),
            pl.BlockSpec((BLOCK, 256), lambda i: (0, VA_OFF // 256)),
        ],
        out_specs=pl.BlockSpec((None, SEQ, 1024), lambda i: (i, 0, 0)),
        out_shape=jax.ShapeDtypeStruct((b, SEQ, 1024), BF16),
        scratch_shapes=[
            pltpu.VMEM((2 * SWA_KV_HEADS, SEQ, LANES), BF16),
            pltpu.VMEM((2 * SWA_KV_HEADS, N_META, LANES), BF16),
            pltpu.VMEM((SWA_KV_HEADS, SWA_VROWS, KEYS_EXT), BF16),
        ],
        compiler_params=_cparams(1),
        name="swa_attn",
    )(sink, proj3, proj3, proj3, proj_lead, proj_lead)


DIFF_VROWS = 2 * HEAD_DIM + BF16_ROWS
DIFF_KCHUNK = 512


def _diff_kernel(lamv_ref, gain_ref, q_ref, k_ref, v_ref, km_ref, vm_ref, o_ref, kx_ref, vt_ref,
                 s0_ref, s1_ref, e0_ref, e1_ref, *, lambda_init, tq):
    s_refs, e_refs = (s0_ref, s1_ref), (e0_ref, e1_ref)

    @pl.when(pl.program_id(2) == 0)
    def _():
        lo = lax.broadcasted_iota(jnp.int32, (1, LANES), 1) < HEAD_DIM
        zero = jnp.zeros((), BF16)

        def put_k(kk, sl):
            kx_ref[0, sl, :] = jnp.where(lo, kk, zero)
            kx_ref[1, sl, :] = jnp.where(lo, zero, kk)
        for c in range(SEQ // 512):
            put_k(k_ref[c * 512:(c + 1) * 512, :], slice(c * 512, (c + 1) * 512))
            for cc in range(2):
                sl = slice(c * 512 + cc * 256, c * 512 + (cc + 1) * 256)
                vt_ref[0:2 * HEAD_DIM, sl] = v_ref[sl, :].astype(F32).T.astype(BF16)
        put_k(km_ref[META_PAD:BLOCK, :], slice(SEQ, KEYS_VALID))
        vt_ref[0:2 * HEAD_DIM, SEQ:KEYS_EXT] = vm_ref[...].astype(F32).T.astype(BF16)
        vt_ref[2 * HEAD_DIM:DIFF_VROWS, :] = jnp.ones((BF16_ROWS, KEYS_EXT), BF16)
        for e_ref in e_refs:
            e_ref[SEQ:SEQ + META_PAD, :] = jnp.zeros((META_PAD, tq), BF16)

    lv = lamv_ref[...]
    lam = (jnp.exp(jnp.sum(lv[0:1] * lv[1:2], axis=-1, keepdims=True))
           - jnp.exp(jnp.sum(lv[2:3] * lv[3:4], axis=-1, keepdims=True)) + lambda_init)
    q = q_ref[...]
    chunks = [(c * DIFF_KCHUNK, DIFF_KCHUNK) for c in range(SEQ // DIFF_KCHUNK)] + [(SEQ, N_META)]

    def scores(mp, k0, kn, m):
        s = _nt_dot(kx_ref[mp, k0:k0 + kn, :], q)
        s_refs[mp][k0:k0 + kn, :] = s
        mc = jnp.max(s, axis=0, keepdims=True)
        return mc if m is None else jnp.maximum(m, mc)

    def exps(mp, k0, kn, m):
        e0 = k0 if k0 < SEQ else SEQ + META_PAD
        e_refs[mp][e0:e0 + kn, :] = jnp.exp2(s_refs[mp][k0:k0 + kn, :] - m).astype(BF16)

    def values(mp):
        return jnp.dot(vt_ref[...], e_refs[mp][...], preferred_element_type=F32)

    m0 = m1 = None
    for ch in chunks:
        m0 = scores(0, *ch, m0)
    for ch in chunks:
        m1 = scores(1, *ch, m1)
        exps(0, *ch, m0)
    o0 = values(0)
    for ch in chunks:
        exps(1, *ch, m1)
    o1 = values(1)
    outs = [o0, o1]

    d = 2 * HEAD_DIM
    r1 = 1.0 / outs[0][d:d + 1]
    r2 = lam / outs[1][d:d + 1]
    o = (outs[0][0:d] * r1 - outs[1][0:d] * r2).T
    y = o * lax.rsqrt(jnp.mean(o * o, axis=-1, keepdims=True) + RMS_EPS) * gain_ref[...]
    o_ref[...] = (y * (1.0 - lambda_init)).astype(BF16)


def _diff(lamv, gain, proj3, proj_lead, *, lambda_init, tq):
    b = proj3.shape[0]
    qb, kb, vb = QB_OFF // LANES, KB_OFF // LANES, VB_OFF // LANES
    kern = functools.partial(_diff_kernel, lambda_init=lambda_init, tq=tq)
    return pl.pallas_call(
        kern,
        grid=(b, DIFF_HEADS, SEQ // tq),
        in_specs=[
            pl.BlockSpec((4, HEAD_DIM), lambda i, h, t: (0, 0)),
            pl.BlockSpec((1, LANES), lambda i, h, t: (0, 0)),
            pl.BlockSpec((None, tq, LANES), lambda i, h, t: (i, t, qb + h)),
            pl.BlockSpec((None, SEQ, LANES), lambda i, h, t: (i, 0, kb + h)),
            pl.BlockSpec((None, SEQ, LANES), lambda i, h, t: (i, 0, vb + h)),
            pl.BlockSpec((BLOCK, LANES), lambda i, h, t: (0, kb + h)),
            pl.BlockSpec((BLOCK, LANES), lambda i, h, t: (0, vb + h)),
        ],
        out_specs=pl.BlockSpec((None, tq, LANES), lambda i, h, t: (i, t, h)),
        out_shape=jax.ShapeDtypeStruct((b, SEQ, DIFF_HEADS * LANES), BF16),
        scratch_shapes=[
            pltpu.VMEM((2, KEYS_VALID, LANES), BF16),
            pltpu.VMEM((DIFF_VROWS, KEYS_EXT), BF16),
            pltpu.VMEM((KEYS_VALID, tq), F32),
            pltpu.VMEM((KEYS_VALID, tq), F32),
            pltpu.VMEM((KEYS_EXT, tq), BF16),
            pltpu.VMEM((KEYS_EXT, tq), BF16),
        ],
        compiler_params=_cparams(3),
        name="diff_attn",
    )(lamv, gain, proj3, proj3, proj3, proj_lead, proj_lead)


def _merge_kernel(x_ref, os_ref, od_ref, gs_ref, gd_ref, ws_ref, wd_ref, wo_ref, gain_ref, o_ref):
    a = jnp.dot(os_ref[...], ws_ref[...], preferred_element_type=F32)
    b = jnp.dot(od_ref[...], wd_ref[...], preferred_element_type=F32)
    merged = gs_ref[...].astype(F32) * a + gd_ref[...].astype(F32) * b
    y = jnp.dot(merged.astype(BF16), wo_ref[...], preferred_element_type=F32)
    o_ref[...] = x_ref[...] + (y * lax.rsqrt(jnp.mean(y * y, axis=-1, keepdims=True) + RMS_EPS)) * gain_ref[...]


def _merge(x2d, o_swa, o_diff, proj, ws, wd, wo, gain, *, tm):
    n = x2d.shape[0]
    wspec = _resident((D_MODEL, D_MODEL))
    return pl.pallas_call(
        _merge_kernel,
        grid=(n // tm,),
        in_specs=[
            pl.BlockSpec((tm, D_MODEL), lambda i: (i, 0)),
            pl.BlockSpec((tm, D_MODEL), lambda i: (i, 0)),
            pl.BlockSpec((tm, D_MODEL), lambda i: (i, 0)),
            pl.BlockSpec((tm, D_MODEL), lambda i: (i, G_OFF // D_MODEL)),
            pl.BlockSpec((tm, D_MODEL), lambda i: (i, G_OFF // D_MODEL + 1)),
            wspec, wspec, wspec,
            _resident((1, D_MODEL)),
        ],
        out_specs=pl.BlockSpec((tm, D_MODEL), lambda i: (i, 0)),
        out_shape=jax.ShapeDtypeStruct((n, D_MODEL), F32),
        compiler_params=_cparams(1),
        name="merge_out",
    )(x2d, o_swa, o_diff, proj, proj, ws, wd, wo, gain)


FF_CHUNK = 256


def _ffn_kernel(h_ref, g1_ref, wi_ref, wo_ref, g2_ref, o_ref, f_ref):
    h = h_ref[...]
    u = ((h * lax.rsqrt(jnp.mean(h * h, axis=-1, keepdims=True) + RMS_EPS)) * g1_ref[...]).astype(BF16)
    for c in range(D_FF // FF_CHUNK):
        gate = jnp.dot(u, wi_ref[:, c * FF_CHUNK:(c + 1) * FF_CHUNK], preferred_element_type=F32)
        up = jnp.dot(u, wi_ref[:, D_FF + c * FF_CHUNK:D_FF + (c + 1) * FF_CHUNK], preferred_element_type=F32)
        f_ref[:, c * FF_CHUNK:(c + 1) * FF_CHUNK] = (jax.nn.silu(gate) * up).astype(BF16)
    f = jnp.dot(f_ref[...], wo_ref[...], preferred_element_type=F32)
    o_ref[...] = h + (f * lax.rsqrt(jnp.mean(f * f, axis=-1, keepdims=True) + RMS_EPS)) * g2_ref[...]


def _ffn(h2d, g1, wi, wo, g2, *, tm):
    n = h2d.shape[0]
    return pl.pallas_call(
        _ffn_kernel,
        grid=(n // tm,),
        in_specs=[
            pl.BlockSpec((tm, D_MODEL), lambda i: (i, 0)),
            _resident((1, D_MODEL)),
            _resident((D_MODEL, 2 * D_FF)),
            _resident((D_FF, D_MODEL)),
            _resident((1, D_MODEL)),
        ],
        out_specs=pl.BlockSpec((tm, D_MODEL), lambda i: (i, 0)),
        out_shape=jax.ShapeDtypeStruct((n, D_MODEL), F32),
        scratch_shapes=[pltpu.VMEM((tm, D_FF), BF16)],
        compiler_params=_cparams(1),
        name="ffn",
    )(h2d, g1, wi, wo, g2)


def _permute_cols(a):
    qa, ka, va, rest = a[..., :1024], a[..., 1024:1280], a[..., 1280:1536], a[..., 1536:]
    return jnp.concatenate([qa, rest, ka, va], axis=-1)


def kernel(x, meta_tokens, pre_mix_gain, w_in, b_gate, attn_sink, lambda_q1, lambda_k1, lambda_q2, lambda_k2,
           diff_subln_gain, w_branch_swa, w_branch_diff, w_out, post_mix_gain, pre_ffn_gain, w_ffn_in, w_ffn_out,
           post_ffn_gain):
    bsz = x.shape[0]
    if w_in.shape[0] != 1:
        raise NotImplementedError("DEPTH > 1 is not supported")
    l = 0
    lambda_init = 0.8 - 0.6 * math.exp(-0.3 * l)
    row = lambda a: a.reshape(1, -1).astype(F32)

    x2d = x.reshape(bsz * SEQ, D_MODEL)
    lead = jnp.concatenate([jnp.zeros((META_PAD, D_MODEL), x.dtype), meta_tokens.astype(x.dtype)], axis=0)
    real_tables = _rope_tables(jnp.arange(SEQ, dtype=jnp.int32) + N_META)
    lead_tables = _rope_tables(jnp.arange(BLOCK, dtype=jnp.int32) - META_PAD)

    w = _permute_cols(w_in[l]).astype(BF16)
    proj = _inproj(x2d, row(pre_mix_gain[l]), w, row(b_gate[l]), *real_tables, tm=512)
    proj_lead = _inproj(lead, row(pre_mix_gain[l]), w, row(b_gate[l]), *lead_tables, tm=BLOCK)
    proj3 = proj.reshape(bsz, SEQ, IN_COLS)

    o_swa = _swa(row(attn_sink[l]), proj3, proj_lead)
    lamv = jnp.stack([lambda_q1[l], lambda_k1[l], lambda_q2[l], lambda_k2[l]]).astype(F32)
    o_diff = _diff(lamv, row(diff_subln_gain[l]), proj3, proj_lead, lambda_init=lambda_init, tq=512)

    h2d = _merge(x2d, o_swa.reshape(bsz * SEQ, D_MODEL), o_diff.reshape(bsz * SEQ, D_MODEL), proj,
                 w_branch_swa[l].astype(BF16), w_branch_diff[l].astype(BF16), w_out[l].astype(BF16),
                 row(post_mix_gain[l]), tm=512)
    h2d = _ffn(h2d, row(pre_ffn_gain[l]), w_ffn_in[l].astype(BF16), w_ffn_out[l].astype(BF16),
               row(post_ffn_gain[l]), tm=512)
    return h2d.reshape(bsz, SEQ, D_MODEL)
```

```python
import functools
import math

import jax
import jax.numpy as jnp
from jax import lax
from jax.experimental import pallas as pl
from jax.experimental.pallas import tpu as pltpu

F32 = jnp.float32
BF16 = jnp.bfloat16

D_MODEL = 1024
SEQ = 2048
HEAD_DIM = 64
HALF = HEAD_DIM // 2
BLOCK = 128
N_META = 16
META_PAD = BLOCK - N_META
WINDOW = 128
ROPE_THETA = 10000.0
RMS_EPS = 1e-6
NEG_INF = -1e30
SWA_Q_HEADS = 16
SWA_KV_HEADS = 4
DIFF_HEADS = 8
D_FF = 2816
LANES = 128
F32_ROWS = 8
BF16_ROWS = 16

QA_OFF, QB_OFF, KB_OFF, VB_OFF, G_OFF, KA_OFF, VA_OFF = 0, 1024, 2048, 3072, 4096, 6144, 6400
IN_COLS = 6656
PROJ_GROUP = 512
BAND = 3 * BLOCK
KEYS_EXT = SEQ + BLOCK
KEYS_VALID = SEQ + N_META

LOG2E = math.log2(math.e)
Q_SCALE = HEAD_DIM ** -0.5 * LOG2E

VMEM_LIMIT = 56 * 1024 * 1024


def _cparams(n_axes, flags=None):
    return pltpu.CompilerParams(dimension_semantics=("arbitrary",) * n_axes, vmem_limit_bytes=VMEM_LIMIT,
                                flags=flags)


def _nt_dot(a, b):
    return lax.dot_general(a, b, (((1,), (1,)), ((), ())), preferred_element_type=F32)


def _resident(shape):
    return pl.BlockSpec(shape, lambda *_: (0,) * len(shape), pipeline_mode=pl.Buffered(1))


def _col_kind(col):
    if col < KB_OFF:
        return "rope", Q_SCALE
    if col < VB_OFF:
        return "rope", 1.0
    if col < G_OFF:
        return "plain", 1.0
    if col < KA_OFF:
        return "gate", 1.0
    if col < VA_OFF:
        return "rope", 1.0
    return "plain", 1.0


def _inproj_kernel(x_ref, g_ref, w_ref, b_ref, cos_ref, sa_ref, sb_ref, o_ref):
    xx = x_ref[...]
    ms = jnp.mean(xx * xx, axis=-1, keepdims=True)
    u = ((xx * lax.rsqrt(ms + RMS_EPS)) * g_ref[...]).astype(BF16)
    cos, sa, sb = cos_ref[...], sa_ref[...], sb_ref[...]
    for grp in range(IN_COLS // PROJ_GROUP):
        c0 = grp * PROJ_GROUP
        acc = jnp.dot(u, w_ref[:, c0:c0 + PROJ_GROUP], preferred_element_type=F32)
        for ci in range(PROJ_GROUP // LANES):
            col = c0 + ci * LANES
            t = acc[:, ci * LANES:(ci + 1) * LANES]
            kind, scale = _col_kind(col)
            if kind == "rope":
                t = t * cos + pltpu.roll(t, LANES - HALF, 1) * sa + pltpu.roll(t, HALF, 1) * sb
                if scale != 1.0:
                    t = t * scale
            elif kind == "gate":
                t = jax.nn.sigmoid(t + b_ref[:, col - G_OFF:col - G_OFF + LANES])
            o_ref[:, col:col + LANES] = t.astype(BF16)


def _inproj(x2d, gain, w, b_gate, cos, sa, sb, *, tm):
    n = x2d.shape[0]
    nt = cos.shape[0] // tm
    table = pl.BlockSpec((tm, LANES), lambda i: (i % nt, 0))
    return pl.pallas_call(
        _inproj_kernel,
        grid=(n // tm,),
        in_specs=[
            pl.BlockSpec((tm, D_MODEL), lambda i: (i, 0)),
            _resident((1, D_MODEL)),
            _resident((D_MODEL, IN_COLS)),
            _resident((1, 2 * D_MODEL)),
            table, table, table,
        ],
        out_specs=pl.BlockSpec((tm, IN_COLS), lambda i: (i, 0)),
        out_shape=jax.ShapeDtypeStruct((n, IN_COLS), BF16),
        compiler_params=_cparams(1),
        name="inproj",
    )(x2d, gain, w, b_gate, cos, sa, sb)


def _rope_tables(pos):
    inv_freq = ROPE_THETA ** (-jnp.arange(0, HEAD_DIM, 2, dtype=F32) / HEAD_DIM)
    ang = pos.astype(F32)[:, None] * inv_freq[None, :]
    lane = jnp.arange(LANES)
    cos = jnp.cos(ang)[:, lane % HALF]
    sin = jnp.sin(ang)[:, lane % HALF]
    first = (lane % HEAD_DIM) < HALF
    return cos, jnp.where(first, -sin, 0.0), jnp.where(first, 0.0, sin)


SWA_VROWS = HEAD_DIM + BF16_ROWS
SWA_KROWS = BAND + N_META


def _swa_kernel(sink_ref, q_ref, k_ref, v_ref, km_ref, vm_ref, o_ref, kx_ref, kmx_ref, vt_ref):
    lo = lax.broadcasted_iota(jnp.int32, (1, LANES), 1) < HEAD_DIM

    def expand_k(src, dst, sl_src, sl_dst):
        for lb in range(2):
            blk = src[sl_src, lb * LANES:(lb + 1) * LANES].astype(F32)
            rolled = pltpu.roll(blk, HEAD_DIM, 1)
            g0, g1 = 2 * lb, 2 * lb + 1
            dst[2 * g0, sl_dst, :] = jnp.where(lo, blk, 0.0).astype(BF16)
            dst[2 * g0 + 1, sl_dst, :] = jnp.where(lo, 0.0, rolled).astype(BF16)
            dst[2 * g1, sl_dst, :] = jnp.where(lo, rolled, 0.0).astype(BF16)
            dst[2 * g1 + 1, sl_dst, :] = jnp.where(lo, 0.0, blk).astype(BF16)

    def put_vt(src, sl_src, sl_dst):
        vt = src[sl_src, :].astype(F32).T
        for g in range(SWA_KV_HEADS):
            vt_ref[g, 0:HEAD_DIM, sl_dst] = vt[g * HEAD_DIM:(g + 1) * HEAD_DIM].astype(BF16)

    def prep_body(c, carry):
        sl = pl.ds(pl.multiple_of(c * 256, 256), 256)
        expand_k(k_ref, kx_ref, sl, sl)
        put_vt(v_ref, sl, sl)
        return carry
    lax.fori_loop(0, SEQ // 256, prep_body, 0)
    expand_k(km_ref, kmx_ref, slice(META_PAD, BLOCK), slice(None))
    put_vt(vm_ref, slice(None), slice(SEQ, KEYS_EXT))
    vt_ref[:, HEAD_DIM:SWA_VROWS, :] = jnp.ones((SWA_KV_HEADS, BF16_ROWS, KEYS_EXT), BF16)

    krow = lax.broadcasted_iota(jnp.int32, (SWA_KROWS, 2 * LANES), 0)
    qcol = lax.broadcasted_iota(jnp.int32, (SWA_KROWS, 2 * LANES), 1) % LANES
    first_blk = lax.broadcasted_iota(jnp.int32, (1, 2 * LANES), 1) < LANES

    def body(r, carry):
        q0 = pl.multiple_of(r * BLOCK, BLOCK)
        start = pl.multiple_of(jnp.clip(r * BLOCK - BLOCK, 0, SEQ - BAND), BLOCK)
        visible = (krow >= BAND) | (jnp.abs((start + krow) - (q0 + qcol)) <= WINDOW)
        bias = jnp.where(visible, 0.0, NEG_INF).astype(F32)
        filler_rows = jnp.zeros((META_PAD, 2 * LANES), BF16)

        chains = [(g, v) for g in range(SWA_KV_HEADS) for v in range(2)]

        def scores(g, v):
            lb0, lb1 = 2 * g, 2 * g + 1
            qq = jnp.concatenate([q_ref[pl.ds(q0, BLOCK), lb0 * LANES:(lb0 + 1) * LANES],
                                  q_ref[pl.ds(q0, BLOCK), lb1 * LANES:(lb1 + 1) * LANES]], axis=0)
            kcat = jnp.concatenate([kx_ref[2 * g + v, pl.ds(start, BAND), :], kmx_ref[2 * g + v]], axis=0)
            sink = jnp.where(first_blk, sink_ref[0, 2 * lb0 + v], sink_ref[0, 2 * lb1 + v]) * LOG2E
            s = _nt_dot(kcat, qq) + bias
            m = jnp.maximum(jnp.max(s, axis=0, keepdims=True), sink)
            return s, m, sink

        def probs(s, m, sink):
            p = jnp.exp2(s - m).astype(BF16)
            return jnp.concatenate([p[0:BAND], filler_rows, p[BAND:SWA_KROWS]], axis=0), jnp.exp2(sink - m)

        def values(g, pt, psink):
            vcat = jnp.concatenate([vt_ref[g, :, pl.ds(start, BAND)], vt_ref[g, :, SEQ:KEYS_EXT]], axis=1)
            o = jnp.dot(vcat, pt, preferred_element_type=F32)
            return o[0:HEAD_DIM] * (1.0 / (o[HEAD_DIM:HEAD_DIM + 1] + psink))

        st_a, st_b, st_c = {}, {}, {}
        for step in range(len(chains) + 2):
            if step < len(chains):
                st_a[step] = scores(*chains[step])
            if 0 <= step - 1 < len(chains):
                st_b[step - 1] = probs(*st_a.pop(step - 1))
            if 0 <= step - 2 < len(chains):
                g, v = chains[step - 2]
                st_c[(g, v)] = values(g, *st_b.pop(step - 2))
                if v == 1:
                    ot = jnp.concatenate([st_c.pop((g, 0)), st_c.pop((g, 1))], axis=0)
                    for half, lb in enumerate((2 * g, 2 * g + 1)):
                        o_ref[pl.ds(q0, BLOCK), lb * LANES:(lb + 1) * LANES] = (
                            ot[:, half * LANES:(half + 1) * LANES].T.astype(BF16))
        return carry
    lax.fori_loop(0, SEQ // BLOCK, body, 0)


def _swa(sink, proj3, proj_lead):
    b = proj3.shape[0]
    return pl.pallas_call(
        _swa_kernel,
        grid=(b,),
        in_specs=[
            pl.BlockSpec(memory_space=pltpu.SMEM),
            pl.BlockSpec((None, SEQ, 1024), lambda i: (i, 0, QA_OFF // 1024)),
            pl.BlockSpec((None, SEQ, 256), lambda i: (i, 0, KA_OFF // 256)),
            pl.BlockSpec((None, SEQ, 256), lambda i: (i, 0, VA_OFF // 256)),
            pl.BlockSpec((BLOCK, 256), lambda i: (0, KA_OFF // 256)),
            pl.BlockSpec((BLOCK, 256), lambda i: (0, VA_OFF // 256)),
        ],
        out_specs=pl.BlockSpec((None, SEQ, 1024), lambda i: (i, 0, 0)),
        out_shape=jax.ShapeDtypeStruct((b, SEQ, 1024), BF16),
        scratch_shapes=[
            pltpu.VMEM((2 * SWA_KV_HEADS, SEQ, LANES), BF16),
            pltpu.VMEM((2 * SWA_KV_HEADS, N_META, LANES), BF16),
            pltpu.VMEM((SWA_KV_HEADS, SWA_VROWS, KEYS_EXT), BF16),
        ],
        compiler_params=_cparams(1),
        name="swa_attn",
    )(sink, proj3, proj3, proj3, proj_lead, proj_lead)


DIFF_VROWS = 2 * HEAD_DIM + BF16_ROWS
DIFF_KCHUNK = 512


def _diff_kernel(lamv_ref, gain_ref, q_ref, k_ref, v_ref, km_ref, vm_ref, o_ref, kx_ref, vt_ref,
                 s0_ref, s1_ref, e0_ref, e1_ref, *, lambda_init, tq):
    s_refs, e_refs = (s0_ref, s1_ref), (e0_ref, e1_ref)

    @pl.when(pl.program_id(2) == 0)
    def _():
        lo = lax.broadcasted_iota(jnp.int32, (1, LANES), 1) < HEAD_DIM
        zero = jnp.zeros((), BF16)

        def put_k(kk, sl):
            kx_ref[0, sl, :] = jnp.where(lo, kk, zero)
            kx_ref[1, sl, :] = jnp.where(lo, zero, kk)
        for c in range(SEQ // 512):
            put_k(k_ref[c * 512:(c + 1) * 512, :], slice(c * 512, (c + 1) * 512))
            for cc in range(2):
                sl = slice(c * 512 + cc * 256, c * 512 + (cc + 1) * 256)
                vt_ref[0:2 * HEAD_DIM, sl] = v_ref[sl, :].astype(F32).T.astype(BF16)
        put_k(km_ref[META_PAD:BLOCK, :], slice(SEQ, KEYS_VALID))
        vt_ref[0:2 * HEAD_DIM, SEQ:KEYS_EXT] = vm_ref[...].astype(F32).T.astype(BF16)
        vt_ref[2 * HEAD_DIM:DIFF_VROWS, :] = jnp.ones((BF16_ROWS, KEYS_EXT), BF16)
        for e_ref in e_refs:
            e_ref[SEQ:SEQ + META_PAD, :] = jnp.zeros((META_PAD, tq), BF16)

    lv = lamv_ref[...]
    lam = (jnp.exp(jnp.sum(lv[0:1] * lv[1:2], axis=-1, keepdims=True))
           - jnp.exp(jnp.sum(lv[2:3] * lv[3:4], axis=-1, keepdims=True)) + lambda_init)
    q = q_ref[...]
    chunks = [(c * DIFF_KCHUNK, DIFF_KCHUNK) for c in range(SEQ // DIFF_KCHUNK)] + [(SEQ, N_META)]

    t = pl.program_id(2)
    half_st = t & 1
    half_ld = t - ((t >> 1) << 1)

    def scores(mp, k0, kn, acc):
        s = _nt_dot(kx_ref[mp, k0:k0 + kn, :], q)
        s_refs[mp][half_st, k0:k0 + kn, :] = s
        acc = list(acc)
        for i, r in enumerate(range(0, kn, F32_ROWS)):
            x = s[r:r + F32_ROWS, :]
            j = i % len(acc)
            acc[j] = x if acc[j] is None else jnp.maximum(acc[j], x)
        return acc

    def col_max(acc):
        m8 = functools.reduce(jnp.maximum, [a for a in acc if a is not None])
        return jnp.broadcast_to(jnp.max(m8, axis=0, keepdims=True), (BF16_ROWS, tq))

    def exps(mp, k0, kn, m16):
        e0 = k0 if k0 < SEQ else SEQ + META_PAD
        for r in range(0, kn, BF16_ROWS):
            x = s_refs[mp][half_ld, k0 + r:k0 + r + BF16_ROWS, :]
            e_refs[mp][e0 + r:e0 + r + BF16_ROWS, :] = jnp.exp2(x - m16).astype(BF16)

    def values(mp, k0=0, kn=KEYS_EXT):
        kn = kn if k0 < SEQ else BLOCK
        return jnp.dot(vt_ref[:, k0:k0 + kn], e_refs[mp][k0:k0 + kn, :], preferred_element_type=F32)

    a0 = a1 = (None, None)
    for ch in chunks:
        a0 = scores(0, *ch, a0)
    m0 = col_max(a0)
    for ch in chunks:
        a1 = scores(1, *ch, a1)
        exps(0, *ch, m0)
    m1 = col_max(a1)
    o0 = values(0)
    parts = []
    for ch in chunks:
        exps(1, *ch, m1)
        parts.append(values(1, *ch))
    o1 = functools.reduce(lambda a, b: a + b, parts)
    outs = [o0, o1]

    d = 2 * HEAD_DIM
    r1 = 1.0 / outs[0][d:d + 1]
    r2 = lam / outs[1][d:d + 1]
    o = (outs[0][0:d] * r1 - outs[1][0:d] * r2).T
    y = o * lax.rsqrt(jnp.mean(o * o, axis=-1, keepdims=True) + RMS_EPS) * gain_ref[...]
    o_ref[...] = (y * (1.0 - lambda_init)).astype(BF16)


def _diff(lamv, gain, proj3, proj_lead, *, lambda_init, tq):
    b = proj3.shape[0]
    qb, kb, vb = QB_OFF // LANES, KB_OFF // LANES, VB_OFF // LANES
    kern = functools.partial(_diff_kernel, lambda_init=lambda_init, tq=tq)
    return pl.pallas_call(
        kern,
        grid=(b, DIFF_HEADS, SEQ // tq),
        in_specs=[
            pl.BlockSpec((4, HEAD_DIM), lambda i, h, t: (0, 0)),
            pl.BlockSpec((1, LANES), lambda i, h, t: (0, 0)),
            pl.BlockSpec((None, tq, LANES), lambda i, h, t: (i, t, qb + h)),
            pl.BlockSpec((None, SEQ, LANES), lambda i, h, t: (i, 0, kb + h)),
            pl.BlockSpec((None, SEQ, LANES), lambda i, h, t: (i, 0, vb + h)),
            pl.BlockSpec((BLOCK, LANES), lambda i, h, t: (0, kb + h)),
            pl.BlockSpec((BLOCK, LANES), lambda i, h, t: (0, vb + h)),
        ],
        out_specs=pl.BlockSpec((None, tq, LANES), lambda i, h, t: (i, t, h)),
        out_shape=jax.ShapeDtypeStruct((b, SEQ, DIFF_HEADS * LANES), BF16),
        scratch_shapes=[
            pltpu.VMEM((2, KEYS_VALID, LANES), BF16),
            pltpu.VMEM((DIFF_VROWS, KEYS_EXT), BF16),
            pltpu.VMEM((2, KEYS_VALID, tq), F32),
            pltpu.VMEM((2, KEYS_VALID, tq), F32),
            pltpu.VMEM((KEYS_EXT, tq), BF16),
            pltpu.VMEM((KEYS_EXT, tq), BF16),
        ],
        compiler_params=_cparams(3),
        name="diff_attn",
    )(lamv, gain, proj3, proj3, proj3, proj_lead, proj_lead)


def _merge_kernel(x_ref, os_ref, od_ref, gs_ref, gd_ref, ws_ref, wd_ref, wo_ref, gain_ref, o_ref):
    a = jnp.dot(os_ref[...], ws_ref[...], preferred_element_type=F32)
    b = jnp.dot(od_ref[...], wd_ref[...], preferred_element_type=F32)
    merged = gs_ref[...].astype(F32) * a + gd_ref[...].astype(F32) * b
    y = jnp.dot(merged.astype(BF16), wo_ref[...], preferred_element_type=F32)
    o_ref[...] = x_ref[...] + (y * lax.rsqrt(jnp.mean(y * y, axis=-1, keepdims=True) + RMS_EPS)) * gain_ref[...]


def _merge(x2d, o_swa, o_diff, proj, ws, wd, wo, gain, *, tm):
    n = x2d.shape[0]
    wspec = _resident((D_MODEL, D_MODEL))
    return pl.pallas_call(
        _merge_kernel,
        grid=(n // tm,),
        in_specs=[
            pl.BlockSpec((tm, D_MODEL), lambda i: (i, 0)),
            pl.BlockSpec((tm, D_MODEL), lambda i: (i, 0)),
            pl.BlockSpec((tm, D_MODEL), lambda i: (i, 0)),
            pl.BlockSpec((tm, D_MODEL), lambda i: (i, G_OFF // D_MODEL)),
            pl.BlockSpec((tm, D_MODEL), lambda i: (i, G_OFF // D_MODEL + 1)),
            wspec, wspec, wspec,
            _resident((1, D_MODEL)),
        ],
        out_specs=pl.BlockSpec((tm, D_MODEL), lambda i: (i, 0)),
        out_shape=jax.ShapeDtypeStruct((n, D_MODEL), F32),
        compiler_params=_cparams(1),
        name="merge_out",
    )(x2d, o_swa, o_diff, proj, proj, ws, wd, wo, gain)


FF_CHUNK = 256


def _ffn_kernel(h_ref, g1_ref, wi_ref, wo_ref, g2_ref, o_ref, f_ref):
    h = h_ref[...]
    u = ((h * lax.rsqrt(jnp.mean(h * h, axis=-1, keepdims=True) + RMS_EPS)) * g1_ref[...]).astype(BF16)
    for c in range(D_FF // FF_CHUNK):
        gate = jnp.dot(u, wi_ref[:, c * FF_CHUNK:(c + 1) * FF_CHUNK], preferred_element_type=F32)
        up = jnp.dot(u, wi_ref[:, D_FF + c * FF_CHUNK:D_FF + (c + 1) * FF_CHUNK], preferred_element_type=F32)
        f_ref[:, c * FF_CHUNK:(c + 1) * FF_CHUNK] = (jax.nn.silu(gate) * up).astype(BF16)
    f = jnp.dot(f_ref[...], wo_ref[...], preferred_element_type=F32)
    o_ref[...] = h + (f * lax.rsqrt(jnp.mean(f * f, axis=-1, keepdims=True) + RMS_EPS)) * g2_ref[...]


def _ffn(h2d, g1, wi, wo, g2, *, tm):
    n = h2d.shape[0]
    return pl.pallas_call(
        _ffn_kernel,
        grid=(n // tm,),
        in_specs=[
            pl.BlockSpec((tm, D_MODEL), lambda i: (i, 0)),
            _resident((1, D_MODEL)),
            _resident((D_MODEL, 2 * D_FF)),
            _resident((D_FF, D_MODEL)),
            _resident((1, D_MODEL)),
        ],
        out_specs=pl.BlockSpec((tm, D_MODEL), lambda i: (i, 0)),
        out_shape=jax.ShapeDtypeStruct((n, D_MODEL), F32),
        scratch_shapes=[pltpu.VMEM((tm, D_FF), BF16)],
        compiler_params=_cparams(1),
        name="ffn",
    )(h2d, g1, wi, wo, g2)


def _permute_cols(a):
    qa, ka, va, rest = a[..., :1024], a[..., 1024:1280], a[..., 1280:1536], a[..., 1536:]
    return jnp.concatenate([qa, rest, ka, va], axis=-1)


def kernel(x, meta_tokens, pre_mix_gain, w_in, b_gate, attn_sink, lambda_q1, lambda_k1, lambda_q2, lambda_k2,
           diff_subln_gain, w_branch_swa, w_branch_diff, w_out, post_mix_gain, pre_ffn_gain, w_ffn_in, w_ffn_out,
           post_ffn_gain):
    bsz = x.shape[0]
    if w_in.shape[0] != 1:
        raise NotImplementedError("DEPTH > 1 is not supported")
    l = 0
    lambda_init = 0.8 - 0.6 * math.exp(-0.3 * l)
    row = lambda a: a.reshape(1, -1).astype(F32)

    x2d = x.reshape(bsz * SEQ, D_MODEL)
    lead = jnp.concatenate([jnp.zeros((META_PAD, D_MODEL), x.dtype), meta_tokens.astype(x.dtype)], axis=0)
    real_tables = _rope_tables(jnp.arange(SEQ, dtype=jnp.int32) + N_META)
    lead_tables = _rope_tables(jnp.arange(BLOCK, dtype=jnp.int32) - META_PAD)

    w = _permute_cols(w_in[l]).astype(BF16)
    proj = _inproj(x2d, row(pre_mix_gain[l]), w, row(b_gate[l]), *real_tables, tm=512)
    proj_lead = _inproj(lead, row(pre_mix_gain[l]), w, row(b_gate[l]), *lead_tables, tm=BLOCK)
    proj3 = proj.reshape(bsz, SEQ, IN_COLS)

    o_swa = _swa(row(attn_sink[l]), proj3, proj_lead)
    lamv = jnp.stack([lambda_q1[l], lambda_k1[l], lambda_q2[l], lambda_k2[l]]).astype(F32)
    o_diff = _diff(lamv, row(diff_subln_gain[l]), proj3, proj_lead, lambda_init=lambda_init, tq=1024)

    h2d = _merge(x2d, o_swa.reshape(bsz * SEQ, D_MODEL), o_diff.reshape(bsz * SEQ, D_MODEL), proj,
                 w_branch_swa[l].astype(BF16), w_branch_diff[l].astype(BF16), w_out[l].astype(BF16),
                 row(post_mix_gain[l]), tm=512)
    h2d = _ffn(h2d, row(pre_ffn_gain[l]), w_ffn_in[l].astype(BF16), w_ffn_out[l].astype(BF16),
               row(post_ffn_gain[l]), tm=512)
    return h2d.reshape(bsz, SEQ, D_MODEL)
```

```python
import functools
import math

import jax
import jax.numpy as jnp
from jax import lax
from jax.experimental import pallas as pl
from jax.experimental.pallas import tpu as pltpu

F32 = jnp.float32
BF16 = jnp.bfloat16

D_MODEL = 1024
SEQ = 2048
HEAD_DIM = 64
HALF = HEAD_DIM // 2
BLOCK = 128
N_META = 16
META_PAD = BLOCK - N_META
WINDOW = 128
ROPE_THETA = 10000.0
RMS_EPS = 1e-6
NEG_INF = -1e30
SWA_Q_HEADS = 16
SWA_KV_HEADS = 4
DIFF_HEADS = 8
D_FF = 2816
LANES = 128
F32_ROWS = 8
BF16_ROWS = 16

QA_OFF, QB_OFF, KB_OFF, VB_OFF, G_OFF, KA_OFF, VA_OFF = 0, 1024, 2048, 3072, 4096, 6144, 6400
IN_COLS = 6656
PROJ_GROUP = 512
BAND = 3 * BLOCK
KEYS_EXT = SEQ + BLOCK
KEYS_VALID = SEQ + N_META

LOG2E = math.log2(math.e)
Q_SCALE = HEAD_DIM ** -0.5 * LOG2E

VMEM_LIMIT = 56 * 1024 * 1024


def _cparams(n_axes, flags=None):
    return pltpu.CompilerParams(dimension_semantics=("arbitrary",) * n_axes, vmem_limit_bytes=VMEM_LIMIT,
                                flags=flags)


def _nt_dot(a, b):
    return lax.dot_general(a, b, (((1,), (1,)), ((), ())), preferred_element_type=F32)


def _resident(shape):
    return pl.BlockSpec(shape, lambda *_: (0,) * len(shape), pipeline_mode=pl.Buffered(1))


def _col_kind(col):
    if col < KB_OFF:
        return "rope", Q_SCALE
    if col < VB_OFF:
        return "rope", 1.0
    if col < G_OFF:
        return "plain", 1.0
    if col < KA_OFF:
        return "gate", 1.0
    if col < VA_OFF:
        return "rope", 1.0
    return "plain", 1.0


def _inproj_kernel(x_ref, g_ref, w_ref, b_ref, cos_ref, sa_ref, sb_ref, o_ref):
    xx = x_ref[...]
    ms = jnp.mean(xx * xx, axis=-1, keepdims=True)
    u = ((xx * lax.rsqrt(ms + RMS_EPS)) * g_ref[...]).astype(BF16)
    cos, sa, sb = cos_ref[...], sa_ref[...], sb_ref[...]
    for grp in range(IN_COLS // PROJ_GROUP):
        c0 = grp * PROJ_GROUP
        acc = jnp.dot(u, w_ref[:, c0:c0 + PROJ_GROUP], preferred_element_type=F32)
        for ci in range(PROJ_GROUP // LANES):
            col = c0 + ci * LANES
            t = acc[:, ci * LANES:(ci + 1) * LANES]
            kind, scale = _col_kind(col)
            if kind == "rope":
                t = t * cos + pltpu.roll(t, LANES - HALF, 1) * sa + pltpu.roll(t, HALF, 1) * sb
                if scale != 1.0:
                    t = t * scale
            elif kind == "gate":
                t = jax.nn.sigmoid(t + b_ref[:, col - G_OFF:col - G_OFF + LANES])
            o_ref[:, col:col + LANES] = t.astype(BF16)


def _inproj(x2d, gain, w, b_gate, cos, sa, sb, *, tm):
    n = x2d.shape[0]
    nt = cos.shape[0] // tm
    table = pl.BlockSpec((tm, LANES), lambda i: (i % nt, 0))
    return pl.pallas_call(
        _inproj_kernel,
        grid=(n // tm,),
        in_specs=[
            pl.BlockSpec((tm, D_MODEL), lambda i: (i, 0)),
            _resident((1, D_MODEL)),
            _resident((D_MODEL, IN_COLS)),
            _resident((1, 2 * D_MODEL)),
            table, table, table,
        ],
        out_specs=pl.BlockSpec((tm, IN_COLS), lambda i: (i, 0)),
        out_shape=jax.ShapeDtypeStruct((n, IN_COLS), BF16),
        compiler_params=_cparams(1),
        name="inproj",
    )(x2d, gain, w, b_gate, cos, sa, sb)


def _rope_tables(pos):
    inv_freq = ROPE_THETA ** (-jnp.arange(0, HEAD_DIM, 2, dtype=F32) / HEAD_DIM)
    ang = pos.astype(F32)[:, None] * inv_freq[None, :]
    lane = jnp.arange(LANES)
    cos = jnp.cos(ang)[:, lane % HALF]
    sin = jnp.sin(ang)[:, lane % HALF]
    first = (lane % HEAD_DIM) < HALF
    return cos, jnp.where(first, -sin, 0.0), jnp.where(first, 0.0, sin)


SWA_VROWS = HEAD_DIM + BF16_ROWS
SWA_KROWS = BAND + N_META
SWA_UNROLL = 2

def _swa_kernel(sink_ref, q_ref, k_ref, v_ref, km_ref, vm_ref, o_ref, kx_ref, kmx_ref, vt_ref):
    lo = lax.broadcasted_iota(jnp.int32, (1, LANES), 1) < HEAD_DIM

    def expand_k(src, dst, sl_src, sl_dst):
        for lb in range(2):
            blk = src[sl_src, lb * LANES:(lb + 1) * LANES].astype(F32)
            rolled = pltpu.roll(blk, HEAD_DIM, 1)
            g0, g1 = 2 * lb, 2 * lb + 1
            dst[2 * g0, sl_dst, :] = jnp.where(lo, blk, 0.0).astype(BF16)
            dst[2 * g0 + 1, sl_dst, :] = jnp.where(lo, 0.0, rolled).astype(BF16)
            dst[2 * g1, sl_dst, :] = jnp.where(lo, rolled, 0.0).astype(BF16)
            dst[2 * g1 + 1, sl_dst, :] = jnp.where(lo, 0.0, blk).astype(BF16)

    def put_vt(src, sl_src, sl_dst):
        vt = src[sl_src, :].astype(F32).T
        for g in range(SWA_KV_HEADS):
            vt_ref[g, 0:HEAD_DIM, sl_dst] = vt[g * HEAD_DIM:(g + 1) * HEAD_DIM].astype(BF16)

    def prep_body(c, carry):
        sl = pl.ds(pl.multiple_of(c * 256, 256), 256)
        expand_k(k_ref, kx_ref, sl, sl)
        put_vt(v_ref, sl, sl)
        return carry
    lax.fori_loop(0, SEQ // 256, prep_body, 0)
    expand_k(km_ref, kmx_ref, slice(META_PAD, BLOCK), slice(None))
    put_vt(vm_ref, slice(None), slice(SEQ, KEYS_EXT))
    vt_ref[:, HEAD_DIM:SWA_VROWS, :] = jnp.ones((SWA_KV_HEADS, BF16_ROWS, KEYS_EXT), BF16)

    krow = lax.broadcasted_iota(jnp.int32, (SWA_KROWS, 2 * LANES), 0)
    kq_delta = krow - lax.broadcasted_iota(jnp.int32, (SWA_KROWS, 2 * LANES), 1) % LANES
    first_blk = lax.broadcasted_iota(jnp.int32, (1, 2 * LANES), 1) < LANES

    filler_rows = jnp.zeros((META_PAD, 2 * LANES), BF16)

    def body(j, carry):
        q0, start, bias = [], [], []
        for i in range(SWA_UNROLL):
            r = j * SWA_UNROLL + i
            q0.append(pl.multiple_of(r * BLOCK, BLOCK))
            start.append(pl.multiple_of(jnp.clip(r * BLOCK - BLOCK, 0, SEQ - BAND), BLOCK))
            visible = (krow >= BAND) | (jnp.abs(kq_delta + (start[i] - q0[i])) <= WINDOW)
            bias.append(jnp.where(visible, 0.0, NEG_INF).astype(F32))

        chains = [(i, g, v) for i in range(SWA_UNROLL) for g in range(SWA_KV_HEADS) for v in range(2)]

        def scores(i, g, v):
            lb0, lb1 = 2 * g, 2 * g + 1
            qq = jnp.concatenate([q_ref[pl.ds(q0[i], BLOCK), lb0 * LANES:(lb0 + 1) * LANES],
                                  q_ref[pl.ds(q0[i], BLOCK), lb1 * LANES:(lb1 + 1) * LANES]], axis=0)
            kcat = jnp.concatenate([kx_ref[2 * g + v, pl.ds(start[i], BAND), :], kmx_ref[2 * g + v]], axis=0)
            sink = jnp.where(first_blk, sink_ref[0, 2 * lb0 + v], sink_ref[0, 2 * lb1 + v]) * LOG2E
            return _nt_dot(kcat, qq), sink, i

        def masked_max(s, sink, i):
            s = s + bias[i]
            return s, jnp.maximum(jnp.max(s, axis=0, keepdims=True), sink), sink

        def probs(s, m, sink):
            p = jnp.exp2(s - m).astype(BF16)
            return jnp.concatenate([p[0:BAND], filler_rows, p[BAND:SWA_KROWS]], axis=0), jnp.exp2(sink - m)

        def values(i, g, pt, psink):
            vcat = jnp.concatenate([vt_ref[g, :, pl.ds(start[i], BAND)], vt_ref[g, :, SEQ:KEYS_EXT]], axis=1)
            o = jnp.dot(vcat, pt, preferred_element_type=F32)
            return o[0:HEAD_DIM] * (1.0 / (o[HEAD_DIM:HEAD_DIM + 1] + psink))

        st_a, st_b, st_c, st_d = {}, {}, {}, {}
        for step in range(len(chains) + 3):
            if step < len(chains):
                st_a[step] = scores(*chains[step])
            if 0 <= step - 1 < len(chains):
                st_b[step - 1] = masked_max(*st_a.pop(step - 1))
            if 0 <= step - 2 < len(chains):
                st_c[step - 2] = probs(*st_b.pop(step - 2))
            if 0 <= step - 3 < len(chains):
                i, g, v = chains[step - 3]
                st_d[v] = values(i, g, *st_c.pop(step - 3))
                if v == 1:
                    ot = jnp.concatenate([st_d.pop(0), st_d.pop(1)], axis=0)
                    for half, lb in enumerate((2 * g, 2 * g + 1)):
                        o_ref[pl.ds(q0[i], BLOCK), lb * LANES:(lb + 1) * LANES] = (
                            ot[:, half * LANES:(half + 1) * LANES].T.astype(BF16))
        return carry
    lax.fori_loop(0, SEQ // (BLOCK * SWA_UNROLL), body, 0)


def _swa(sink, proj3, proj_lead):
    b = proj3.shape[0]
    return pl.pallas_call(
        _swa_kernel,
        grid=(b,),
        in_specs=[
            pl.BlockSpec(memory_space=pltpu.SMEM),
            pl.BlockSpec((None, SEQ, 1024), lambda i: (i, 0, QA_OFF // 1024)),
            pl.BlockSpec((None, SEQ, 256), lambda i: (i, 0, KA_OFF // 256)),
            pl.BlockSpec((None, SEQ, 256), lambda i: (i, 0, VA_OFF // 256)),
            pl.BlockSpec((BLOCK, 256), lambda i: (0, KA_OFF // 256)),
            pl.BlockSpec((BLOCK, 256), lambda i: (0, VA_OFF // 256)),
        ],
        out_specs=pl.BlockSpec((None, SEQ, 1024), lambda i: (i, 0, 0)),
        out_shape=jax.ShapeDtypeStruct((b, SEQ, 1024), BF16),
        scratch_shapes=[
            pltpu.VMEM((2 * SWA_KV_HEADS, SEQ, LANES), BF16),
            pltpu.VMEM((2 * SWA_KV_HEADS, N_META, LANES), BF16),
            pltpu.VMEM((SWA_KV_HEADS, SWA_VROWS, KEYS_EXT), BF16),
        ],
        compiler_params=_cparams(1),
        name="swa_attn",
    )(sink, proj3, proj3, proj3, proj_lead, proj_lead)


DIFF_VROWS = 2 * HEAD_DIM + BF16_ROWS
DIFF_KCHUNK = 512


def _diff_kernel(lamv_ref, gain_ref, q_ref, k_ref, v_ref, km_ref, vm_ref, o_ref, kx_ref, vt_ref,
                 s0_ref, s1_ref, e0_ref, e1_ref, *, lambda_init, tq):
    s_refs, e_refs = (s0_ref, s1_ref), (e0_ref, e1_ref)

    @pl.when(pl.program_id(2) == 0)
    def _():
        lo = lax.broadcasted_iota(jnp.int32, (1, LANES), 1) < HEAD_DIM
        zero = jnp.zeros((), BF16)

        def put_k(kk, sl):
            kx_ref[0, sl, :] = jnp.where(lo, kk, zero)
            kx_ref[1, sl, :] = jnp.where(lo, zero, kk)
        for c in range(SEQ // 512):
            put_k(k_ref[c * 512:(c + 1) * 512, :], slice(c * 512, (c + 1) * 512))
            for cc in range(2):
                sl = slice(c * 512 + cc * 256, c * 512 + (cc + 1) * 256)
                vt_ref[0:2 * HEAD_DIM, sl] = v_ref[sl, :].astype(F32).T.astype(BF16)
        put_k(km_ref[META_PAD:BLOCK, :], slice(SEQ, KEYS_VALID))
        vt_ref[0:2 * HEAD_DIM, SEQ:KEYS_EXT] = vm_ref[...].astype(F32).T.astype(BF16)
        vt_ref[2 * HEAD_DIM:DIFF_VROWS, :] = jnp.ones((BF16_ROWS, KEYS_EXT), BF16)
        for e_ref in e_refs:
            e_ref[SEQ:SEQ + META_PAD, :] = jnp.zeros((META_PAD, tq), BF16)

    lv = lamv_ref[...]
    lam = (jnp.exp(jnp.sum(lv[0:1] * lv[1:2], axis=-1, keepdims=True))
           - jnp.exp(jnp.sum(lv[2:3] * lv[3:4], axis=-1, keepdims=True)) + lambda_init)
    chunks = [(k0, DIFF_KCHUNK if k0 + DIFF_KCHUNK < SEQ else KEYS_VALID - k0) for k0 in range(0, SEQ, DIFF_KCHUNK)]
    hq = tq // 2
    qcols = [slice(0, hq), slice(hq, tq)]

    t = pl.program_id(2)
    half_st = t & 1
    half_ld = t - ((t >> 1) << 1)

    acc, mx, out = {}, {}, {}

    def scores(it, k0, kn):
        qh, mp = it
        s = _nt_dot(kx_ref[mp, k0:k0 + kn, :], q_ref[qcols[qh], :])
        s_refs[mp][half_st, k0:k0 + kn, qcols[qh]] = s
        a = acc.setdefault(it, [None, None])
        for i, r in enumerate(range(0, kn, F32_ROWS)):
            x = s[r:r + F32_ROWS, :]
            a[i % 2] = x if a[i % 2] is None else jnp.maximum(a[i % 2], x)

    def col_max(it):
        m8 = jnp.maximum(*acc.pop(it))
        mx[it] = jnp.broadcast_to(jnp.max(m8, axis=0, keepdims=True), (BF16_ROWS, m8.shape[1]))

    def exps(it, k0, kn):
        qh, mp = it
        for r in range(k0, k0 + kn, BF16_ROWS):
            x = s_refs[mp][half_ld, r:r + BF16_ROWS, qcols[qh]]
            er = r if r < SEQ else r + META_PAD
            e_refs[mp][er:er + BF16_ROWS, qcols[qh]] = jnp.exp2(x - mx[it]).astype(BF16)

    def values(it):
        qh, mp = it
        out[it] = jnp.dot(vt_ref[...], e_refs[mp][:, qcols[qh]], preferred_element_type=F32)

    def finish(qh):
        d = 2 * HEAD_DIM
        o1, o2 = out.pop((qh, 0)), out.pop((qh, 1))
        r1 = 1.0 / o1[d:d + 1]
        r2 = lam / o2[d:d + 1]
        o = (o1[0:d] * r1 - o2[0:d] * r2).T
        y = o * lax.rsqrt(jnp.mean(o * o, axis=-1, keepdims=True) + RMS_EPS) * gain_ref[...]
        o_ref[qcols[qh], :] = (y * (1.0 - lambda_init)).astype(BF16)

    def staged(score_it=None, exp_it=None):
        if exp_it is not None:
            col_max(exp_it)
        for ch in chunks:
            if score_it is not None:
                scores(score_it, *ch)
            if exp_it is not None:
                exps(exp_it, *ch)

    staged(score_it=(0, 0))
    staged(score_it=(0, 1), exp_it=(0, 0))
    staged(score_it=(1, 0), exp_it=(0, 1))
    values((0, 0))
    staged(score_it=(1, 1), exp_it=(1, 0))
    values((0, 1))
    values((1, 0))
    staged(exp_it=(1, 1))
    finish(0)
    values((1, 1))
    finish(1)


def _diff(lamv, gain, proj3, proj_lead, *, lambda_init, tq):
    b = proj3.shape[0]
    qb, kb, vb = QB_OFF // LANES, KB_OFF // LANES, VB_OFF // LANES
    kern = functools.partial(_diff_kernel, lambda_init=lambda_init, tq=tq)
    return pl.pallas_call(
        kern,
        grid=(b, DIFF_HEADS, SEQ // tq),
        in_specs=[
            pl.BlockSpec((4, HEAD_DIM), lambda i, h, t: (0, 0)),
            pl.BlockSpec((1, LANES), lambda i, h, t: (0, 0)),
            pl.BlockSpec((None, tq, LANES), lambda i, h, t: (i, t, qb + h)),
            pl.BlockSpec((None, SEQ, LANES), lambda i, h, t: (i, 0, kb + h)),
            pl.BlockSpec((None, SEQ, LANES), lambda i, h, t: (i, 0, vb + h)),
            pl.BlockSpec((BLOCK, LANES), lambda i, h, t: (0, kb + h)),
            pl.BlockSpec((BLOCK, LANES), lambda i, h, t: (0, vb + h)),
        ],
        out_specs=pl.BlockSpec((None, tq, LANES), lambda i, h, t: (i, t, h)),
        out_shape=jax.ShapeDtypeStruct((b, SEQ, DIFF_HEADS * LANES), BF16),
        scratch_shapes=[
            pltpu.VMEM((2, KEYS_VALID, LANES), BF16),
            pltpu.VMEM((DIFF_VROWS, KEYS_EXT), BF16),
            pltpu.VMEM((2, KEYS_VALID, tq), F32),
            pltpu.VMEM((2, KEYS_VALID, tq), F32),
            pltpu.VMEM((KEYS_EXT, tq), BF16),
            pltpu.VMEM((KEYS_EXT, tq), BF16),
        ],
        compiler_params=_cparams(3),
        name="diff_attn",
    )(lamv, gain, proj3, proj3, proj3, proj_lead, proj_lead)


def _merge_kernel(x_ref, os_ref, od_ref, gs_ref, gd_ref, ws_ref, wd_ref, wo_ref, gain_ref, o_ref):
    a = jnp.dot(os_ref[...], ws_ref[...], preferred_element_type=F32)
    b = jnp.dot(od_ref[...], wd_ref[...], preferred_element_type=F32)
    merged = gs_ref[...].astype(F32) * a + gd_ref[...].astype(F32) * b
    y = jnp.dot(merged.astype(BF16), wo_ref[...], preferred_element_type=F32)
    o_ref[...] = x_ref[...] + (y * lax.rsqrt(jnp.mean(y * y, axis=-1, keepdims=True) + RMS_EPS)) * gain_ref[...]


def _merge(x2d, o_swa, o_diff, proj, ws, wd, wo, gain, *, tm):
    n = x2d.shape[0]
    wspec = _resident((D_MODEL, D_MODEL))
    return pl.pallas_call(
        _merge_kernel,
        grid=(n // tm,),
        in_specs=[
            pl.BlockSpec((tm, D_MODEL), lambda i: (i, 0)),
            pl.BlockSpec((tm, D_MODEL), lambda i: (i, 0)),
            pl.BlockSpec((tm, D_MODEL), lambda i: (i, 0)),
            pl.BlockSpec((tm, D_MODEL), lambda i: (i, G_OFF // D_MODEL)),
            pl.BlockSpec((tm, D_MODEL), lambda i: (i, G_OFF // D_MODEL + 1)),
            wspec, wspec, wspec,
            _resident((1, D_MODEL)),
        ],
        out_specs=pl.BlockSpec((tm, D_MODEL), lambda i: (i, 0)),
        out_shape=jax.ShapeDtypeStruct((n, D_MODEL), F32),
        compiler_params=_cparams(1),
        name="merge_out",
    )(x2d, o_swa, o_diff, proj, proj, ws, wd, wo, gain)


FF_CHUNK = 256


def _ffn_kernel(h_ref, g1_ref, wi_ref, wo_ref, g2_ref, o_ref, f_ref):
    h = h_ref[...]
    u = ((h * lax.rsqrt(jnp.mean(h * h, axis=-1, keepdims=True) + RMS_EPS)) * g1_ref[...]).astype(BF16)
    for c in range(D_FF // FF_CHUNK):
        gate = jnp.dot(u, wi_ref[:, c * FF_CHUNK:(c + 1) * FF_CHUNK], preferred_element_type=F32)
        up = jnp.dot(u, wi_ref[:, D_FF + c * FF_CHUNK:D_FF + (c + 1) * FF_CHUNK], preferred_element_type=F32)
        f_ref[:, c * FF_CHUNK:(c + 1) * FF_CHUNK] = (jax.nn.silu(gate) * up).astype(BF16)
    f = jnp.dot(f_ref[...], wo_ref[...], preferred_element_type=F32)
    o_ref[...] = h + (f * lax.rsqrt(jnp.mean(f * f, axis=-1, keepdims=True) + RMS_EPS)) * g2_ref[...]


def _ffn(h2d, g1, wi, wo, g2, *, tm):
    n = h2d.shape[0]
    return pl.pallas_call(
        _ffn_kernel,
        grid=(n // tm,),
        in_specs=[
            pl.BlockSpec((tm, D_MODEL), lambda i: (i, 0)),
            _resident((1, D_MODEL)),
            _resident((D_MODEL, 2 * D_FF)),
            _resident((D_FF, D_MODEL)),
            _resident((1, D_MODEL)),
        ],
        out_specs=pl.BlockSpec((tm, D_MODEL), lambda i: (i, 0)),
        out_shape=jax.ShapeDtypeStruct((n, D_MODEL), F32),
        scratch_shapes=[pltpu.VMEM((tm, D_FF), BF16)],
        compiler_params=_cparams(1),
        name="ffn",
    )(h2d, g1, wi, wo, g2)


def _permute_cols(a):
    qa, ka, va, rest = a[..., :1024], a[..., 1024:1280], a[..., 1280:1536], a[..., 1536:]
    return jnp.concatenate([qa, rest, ka, va], axis=-1)


def kernel(x, meta_tokens, pre_mix_gain, w_in, b_gate, attn_sink, lambda_q1, lambda_k1, lambda_q2, lambda_k2,
           diff_subln_gain, w_branch_swa, w_branch_diff, w_out, post_mix_gain, pre_ffn_gain, w_ffn_in, w_ffn_out,
           post_ffn_gain):
    bsz = x.shape[0]
    if w_in.shape[0] != 1:
        raise NotImplementedError("DEPTH > 1 is not supported")
    l = 0
    lambda_init = 0.8 - 0.6 * math.exp(-0.3 * l)
    row = lambda a: a.reshape(1, -1).astype(F32)

    x2d = x.reshape(bsz * SEQ, D_MODEL)
    lead = jnp.concatenate([jnp.zeros((META_PAD, D_MODEL), x.dtype), meta_tokens.astype(x.dtype)], axis=0)
    real_tables = _rope_tables(jnp.arange(SEQ, dtype=jnp.int32) + N_META)
    lead_tables = _rope_tables(jnp.arange(BLOCK, dtype=jnp.int32) - META_PAD)

    w = _permute_cols(w_in[l]).astype(BF16)
    proj = _inproj(x2d, row(pre_mix_gain[l]), w, row(b_gate[l]), *real_tables, tm=512)
    proj_lead = _inproj(lead, row(pre_mix_gain[l]), w, row(b_gate[l]), *lead_tables, tm=BLOCK)
    proj3 = proj.reshape(bsz, SEQ, IN_COLS)

    o_swa = _swa(row(attn_sink[l]), proj3, proj_lead)
    lamv = jnp.stack([lambda_q1[l], lambda_k1[l], lambda_q2[l], lambda_k2[l]]).astype(F32)
    o_diff = _diff(lamv, row(diff_subln_gain[l]), proj3, proj_lead, lambda_init=lambda_init, tq=1024)

    h2d = _merge(x2d, o_swa.reshape(bsz * SEQ, D_MODEL), o_diff.reshape(bsz * SEQ, D_MODEL), proj,
                 w_branch_swa[l].astype(BF16), w_branch_diff[l].astype(BF16), w_out[l].astype(BF16),
                 row(post_mix_gain[l]), tm=512)
    h2d = _ffn(h2d, row(pre_ffn_gain[l]), w_ffn_in[l].astype(BF16), w_ffn_out[l].astype(BF16),
               row(post_ffn_gain[l]), tm=512)
    return h2d.reshape(bsz, SEQ, D_MODEL)
```

```python
import functools
import math

import jax
import jax.numpy as jnp
from jax import lax
from jax.experimental import pallas as pl
from jax.experimental.pallas import tpu as pltpu

F32 = jnp.float32
BF16 = jnp.bfloat16

D_MODEL = 1024
SEQ = 2048
HEAD_DIM = 64
HALF = HEAD_DIM // 2
BLOCK = 128
N_META = 16
META_PAD = BLOCK - N_META
WINDOW = 128
ROPE_THETA = 10000.0
RMS_EPS = 1e-6
NEG_INF = -1e30
SWA_Q_HEADS = 16
SWA_KV_HEADS = 4
DIFF_HEADS = 8
D_FF = 2816
LANES = 128
F32_ROWS = 8
BF16_ROWS = 16

QA_OFF, QB_OFF, KB_OFF, VB_OFF, G_OFF, KA_OFF, VA_OFF = 0, 1024, 2048, 3072, 4096, 6144, 6400
IN_COLS = 6656
PROJ_GROUP = 512
BAND = 3 * BLOCK
KEYS_EXT = SEQ + BLOCK
KEYS_VALID = SEQ + N_META

LOG2E = math.log2(math.e)
Q_SCALE = HEAD_DIM ** -0.5 * LOG2E

VMEM_LIMIT = 56 * 1024 * 1024


def _cparams(n_axes, flags=None):
    return pltpu.CompilerParams(dimension_semantics=("arbitrary",) * n_axes, vmem_limit_bytes=VMEM_LIMIT,
                                flags=flags)


def _nt_dot(a, b):
    return lax.dot_general(a, b, (((1,), (1,)), ((), ())), preferred_element_type=F32)


def _resident(shape):
    return pl.BlockSpec(shape, lambda *_: (0,) * len(shape), pipeline_mode=pl.Buffered(1))


def _col_kind(col):
    if col < KB_OFF:
        return "rope", Q_SCALE
    if col < VB_OFF:
        return "rope", 1.0
    if col < G_OFF:
        return "plain", 1.0
    if col < KA_OFF:
        return "gate", 1.0
    if col < VA_OFF:
        return "rope", 1.0
    return "plain", 1.0


def _inproj_kernel(x_ref, g_ref, w_ref, b_ref, cos_ref, sa_ref, sb_ref, o_ref):
    xx = x_ref[...]
    ms = jnp.mean(xx * xx, axis=-1, keepdims=True)
    u = ((xx * lax.rsqrt(ms + RMS_EPS)) * g_ref[...]).astype(BF16)
    cos, sa, sb = cos_ref[...], sa_ref[...], sb_ref[...]
    for grp in range(IN_COLS // PROJ_GROUP):
        c0 = grp * PROJ_GROUP
        acc = jnp.dot(u, w_ref[:, c0:c0 + PROJ_GROUP], preferred_element_type=F32)
        for ci in range(PROJ_GROUP // LANES):
            col = c0 + ci * LANES
            t = acc[:, ci * LANES:(ci + 1) * LANES]
            kind, scale = _col_kind(col)
            if kind == "rope":
                t = t * cos + pltpu.roll(t, LANES - HALF, 1) * sa + pltpu.roll(t, HALF, 1) * sb
                if scale != 1.0:
                    t = t * scale
            elif kind == "gate":
                t = jax.nn.sigmoid(t + b_ref[:, col - G_OFF:col - G_OFF + LANES])
            o_ref[:, col:col + LANES] = t.astype(BF16)


def _inproj(x2d, gain, w, b_gate, cos, sa, sb, *, tm):
    n = x2d.shape[0]
    nt = cos.shape[0] // tm
    table = pl.BlockSpec((tm, LANES), lambda i: (i % nt, 0))
    return pl.pallas_call(
        _inproj_kernel,
        grid=(n // tm,),
        in_specs=[
            pl.BlockSpec((tm, D_MODEL), lambda i: (i, 0)),
            _resident((1, D_MODEL)),
            _resident((D_MODEL, IN_COLS)),
            _resident((1, 2 * D_MODEL)),
            table, table, table,
        ],
        out_specs=pl.BlockSpec((tm, IN_COLS), lambda i: (i, 0)),
        out_shape=jax.ShapeDtypeStruct((n, IN_COLS), BF16),
        compiler_params=_cparams(1),
        name="inproj",
    )(x2d, gain, w, b_gate, cos, sa, sb)


def _rope_tables(pos):
    inv_freq = ROPE_THETA ** (-jnp.arange(0, HEAD_DIM, 2, dtype=F32) / HEAD_DIM)
    ang = pos.astype(F32)[:, None] * inv_freq[None, :]
    lane = jnp.arange(LANES)
    cos = jnp.cos(ang)[:, lane % HALF]
    sin = jnp.sin(ang)[:, lane % HALF]
    first = (lane % HEAD_DIM) < HALF
    return cos, jnp.where(first, -sin, 0.0), jnp.where(first, 0.0, sin)


SWA_VROWS = HEAD_DIM + BF16_ROWS
SWA_KROWS = BAND + N_META
SWA_UNROLL = 4
SWA_VALUE_LAG = 5


def _swa_kernel(sink_ref, q_ref, k_ref, v_ref, km_ref, vm_ref, o_ref, kx_ref, kmx_ref, vt_ref):
    lo = lax.broadcasted_iota(jnp.int32, (1, LANES), 1) < HEAD_DIM

    def expand_k(src, dst, sl_src, sl_dst):
        for lb in range(2):
            blk = src[sl_src, lb * LANES:(lb + 1) * LANES].astype(F32)
            rolled = pltpu.roll(blk, HEAD_DIM, 1)
            g0, g1 = 2 * lb, 2 * lb + 1
            dst[2 * g0, sl_dst, :] = jnp.where(lo, blk, 0.0).astype(BF16)
            dst[2 * g0 + 1, sl_dst, :] = jnp.where(lo, 0.0, rolled).astype(BF16)
            dst[2 * g1, sl_dst, :] = jnp.where(lo, rolled, 0.0).astype(BF16)
            dst[2 * g1 + 1, sl_dst, :] = jnp.where(lo, 0.0, blk).astype(BF16)

    def put_vt(src, sl_src, sl_dst):
        vt = src[sl_src, :].astype(F32).T
        for g in range(SWA_KV_HEADS):
            vt_ref[g, 0:HEAD_DIM, sl_dst] = vt[g * HEAD_DIM:(g + 1) * HEAD_DIM].astype(BF16)

    def prep_body(c, carry):
        sl = pl.ds(pl.multiple_of(c * 256, 256), 256)
        expand_k(k_ref, kx_ref, sl, sl)
        put_vt(v_ref, sl, sl)
        return carry
    lax.fori_loop(0, SEQ // 256, prep_body, 0)
    expand_k(km_ref, kmx_ref, slice(META_PAD, BLOCK), slice(None))
    put_vt(vm_ref, slice(None), slice(SEQ, KEYS_EXT))
    vt_ref[:, HEAD_DIM:SWA_VROWS, :] = jnp.ones((SWA_KV_HEADS, BF16_ROWS, KEYS_EXT), BF16)

    krow = lax.broadcasted_iota(jnp.int32, (SWA_KROWS, 2 * LANES), 0)
    kq_delta = krow - lax.broadcasted_iota(jnp.int32, (SWA_KROWS, 2 * LANES), 1) % LANES
    first_blk = lax.broadcasted_iota(jnp.int32, (1, 2 * LANES), 1) < LANES

    filler_rows = jnp.zeros((META_PAD, 2 * LANES), BF16)

    def body(j, carry):
        q0, start, bias = [], [], []
        for i in range(SWA_UNROLL):
            r = j * SWA_UNROLL + i
            q0.append(pl.multiple_of(r * BLOCK, BLOCK))
            start.append(pl.multiple_of(jnp.clip(r * BLOCK - BLOCK, 0, SEQ - BAND), BLOCK))
            visible = (krow >= BAND) | (jnp.abs(kq_delta + (start[i] - q0[i])) <= WINDOW)
            bias.append(jnp.where(visible, 0.0, NEG_INF).astype(F32))

        chains = [(i, g, v) for i in range(SWA_UNROLL) for g in range(SWA_KV_HEADS) for v in range(2)]

        def scores(i, g, v):
            lb0, lb1 = 2 * g, 2 * g + 1
            qq = jnp.concatenate([q_ref[pl.ds(q0[i], BLOCK), lb0 * LANES:(lb0 + 1) * LANES],
                                  q_ref[pl.ds(q0[i], BLOCK), lb1 * LANES:(lb1 + 1) * LANES]], axis=0)
            kcat = jnp.concatenate([kx_ref[2 * g + v, pl.ds(start[i], BAND), :], kmx_ref[2 * g + v]], axis=0)
            sink = jnp.where(first_blk, sink_ref[0, 2 * lb0 + v], sink_ref[0, 2 * lb1 + v]) * LOG2E
            return _nt_dot(kcat, qq), sink, i

        def masked_max(s, sink, i):
            s = s + bias[i]
            return s, jnp.maximum(jnp.max(s, axis=0, keepdims=True), sink), sink

        def probs(s, m, sink):
            p = jnp.exp2(s - m).astype(BF16)
            return jnp.concatenate([p[0:BAND], filler_rows, p[BAND:SWA_KROWS]], axis=0), jnp.exp2(sink - m)

        def values(i, g, pt, psink):
            vcat = jnp.concatenate([vt_ref[g, :, pl.ds(start[i], BAND)], vt_ref[g, :, SEQ:KEYS_EXT]], axis=1)
            o = jnp.dot(vcat, pt, preferred_element_type=F32)
            return o[0:HEAD_DIM] * (1.0 / (o[HEAD_DIM:HEAD_DIM + 1] + psink))

        st_a, st_b, st_c, st_d = {}, {}, {}, {}
        for step in range(len(chains) + SWA_VALUE_LAG):
            if step < len(chains):
                st_a[step] = scores(*chains[step])
            if 0 <= step - 1 < len(chains):
                st_b[step - 1] = masked_max(*st_a.pop(step - 1))
            if 0 <= step - 2 < len(chains):
                st_c[step - 2] = probs(*st_b.pop(step - 2))
            if 0 <= step - SWA_VALUE_LAG < len(chains):
                i, g, v = chains[step - SWA_VALUE_LAG]
                st_d[v] = values(i, g, *st_c.pop(step - SWA_VALUE_LAG))
                if v == 1:
                    ot = jnp.concatenate([st_d.pop(0), st_d.pop(1)], axis=0)
                    for half, lb in enumerate((2 * g, 2 * g + 1)):
                        o_ref[pl.ds(q0[i], BLOCK), lb * LANES:(lb + 1) * LANES] = (
                            ot[:, half * LANES:(half + 1) * LANES].T.astype(BF16))
        return carry
    lax.fori_loop(0, SEQ // (BLOCK * SWA_UNROLL), body, 0)


def _swa(sink, proj3, proj_lead):
    b = proj3.shape[0]
    return pl.pallas_call(
        _swa_kernel,
        grid=(b,),
        in_specs=[
            pl.BlockSpec(memory_space=pltpu.SMEM),
            pl.BlockSpec((None, SEQ, 1024), lambda i: (i, 0, QA_OFF // 1024)),
            pl.BlockSpec((None, SEQ, 256), lambda i: (i, 0, KA_OFF // 256)),
            pl.BlockSpec((None, SEQ, 256), lambda i: (i, 0, VA_OFF // 256)),
            pl.BlockSpec((BLOCK, 256), lambda i: (0, KA_OFF // 256)),
            pl.BlockSpec((BLOCK, 256), lambda i: (0, VA_OFF // 256)),
        ],
        out_specs=pl.BlockSpec((None, SEQ, 1024), lambda i: (i, 0, 0)),
        out_shape=jax.ShapeDtypeStruct((b, SEQ, 1024), BF16),
        scratch_shapes=[
            pltpu.VMEM((2 * SWA_KV_HEADS, SEQ, LANES), BF16),
            pltpu.VMEM((2 * SWA_KV_HEADS, N_META, LANES), BF16),
            pltpu.VMEM((SWA_KV_HEADS, SWA_VROWS, KEYS_EXT), BF16),
        ],
        compiler_params=_cparams(1),
        name="swa_attn",
    )(sink, proj3, proj3, proj3, proj_lead, proj_lead)


DIFF_VROWS = 2 * HEAD_DIM + BF16_ROWS
DIFF_KCHUNK = 512


def _diff_kernel(lamv_ref, gain_ref, q_ref, k_ref, v_ref, km_ref, vm_ref, o_ref, kx_ref, vt_ref,
                 s0_ref, s1_ref, e0_ref, e1_ref, *, lambda_init, tq):
    s_refs, e_refs = (s0_ref, s1_ref), (e0_ref, e1_ref)

    @pl.when(pl.program_id(2) == 0)
    def _():
        lo = lax.broadcasted_iota(jnp.int32, (1, LANES), 1) < HEAD_DIM
        zero = jnp.zeros((), BF16)

        def put_k(kk, sl):
            kx_ref[0, sl, :] = jnp.where(lo, kk, zero)
            kx_ref[1, sl, :] = jnp.where(lo, zero, kk)
        for c in range(SEQ // 512):
            put_k(k_ref[c * 512:(c + 1) * 512, :], slice(c * 512, (c + 1) * 512))
            for cc in range(2):
                sl = slice(c * 512 + cc * 256, c * 512 + (cc + 1) * 256)
                vt_ref[0:2 * HEAD_DIM, sl] = v_ref[sl, :].astype(F32).T.astype(BF16)
        put_k(km_ref[META_PAD:BLOCK, :], slice(SEQ, KEYS_VALID))
        vt_ref[0:2 * HEAD_DIM, SEQ:KEYS_EXT] = vm_ref[...].astype(F32).T.astype(BF16)
        vt_ref[2 * HEAD_DIM:DIFF_VROWS, :] = jnp.ones((BF16_ROWS, KEYS_EXT), BF16)
        for e_ref in e_refs:
            e_ref[SEQ:SEQ + META_PAD, :] = jnp.zeros((META_PAD, tq), BF16)

    lv = lamv_ref[...]
    lam = (jnp.exp(jnp.sum(lv[0:1] * lv[1:2], axis=-1, keepdims=True))
           - jnp.exp(jnp.sum(lv[2:3] * lv[3:4], axis=-1, keepdims=True)) + lambda_init)
    chunks = [(k0, DIFF_KCHUNK if k0 + DIFF_KCHUNK < SEQ else KEYS_VALID - k0) for k0 in range(0, SEQ, DIFF_KCHUNK)]
    hq = tq // 2
    qcols = [slice(0, hq), slice(hq, tq)]

    t = pl.program_id(2)
    half_st = t & 1
    half_ld = t - ((t >> 1) << 1)

    acc, mx, out = {}, {}, {}

    def scores(it, k0, kn):
        qh, mp = it
        s = _nt_dot(kx_ref[mp, k0:k0 + kn, :], q_ref[qcols[qh], :])
        s_refs[mp][half_st, k0:k0 + kn, qcols[qh]] = s
        a = acc.setdefault(it, [None, None])
        for i, r in enumerate(range(0, kn, F32_ROWS)):
            x = s[r:r + F32_ROWS, :]
            a[i % 2] = x if a[i % 2] is None else jnp.maximum(a[i % 2], x)

    def col_max(it):
        m8 = jnp.maximum(*acc.pop(it))
        mx[it] = jnp.broadcast_to(jnp.max(m8, axis=0, keepdims=True), (BF16_ROWS, m8.shape[1]))

    def exps(it, k0, kn):
        qh, mp = it
        for r in range(k0, k0 + kn, BF16_ROWS):
            x = s_refs[mp][half_ld, r:r + BF16_ROWS, qcols[qh]]
            er = r if r < SEQ else r + META_PAD
            e_refs[mp][er:er + BF16_ROWS, qcols[qh]] = jnp.exp2(x - mx[it]).astype(BF16)

    def values(it):
        qh, mp = it
        out[it] = jnp.dot(vt_ref[...], e_refs[mp][:, qcols[qh]], preferred_element_type=F32)

    def finish(qh):
        d = 2 * HEAD_DIM
        o1, o2 = out.pop((qh, 0)), out.pop((qh, 1))
        r1 = 1.0 / o1[d:d + 1]
        r2 = lam / o2[d:d + 1]
        o = (o1[0:d] * r1 - o2[0:d] * r2).T
        y = o * lax.rsqrt(jnp.mean(o * o, axis=-1, keepdims=True) + RMS_EPS) * gain_ref[...]
        o_ref[qcols[qh], :] = (y * (1.0 - lambda_init)).astype(BF16)

    def staged(score_it=None, exp_it=None):
        if exp_it is not None:
            col_max(exp_it)
        for ch in chunks:
            if score_it is not None:
                scores(score_it, *ch)
            if exp_it is not None:
                exps(exp_it, *ch)

    staged(score_it=(0, 0))
    staged(score_it=(0, 1), exp_it=(0, 0))
    staged(score_it=(1, 0), exp_it=(0, 1))
    values((0, 0))
    staged(score_it=(1, 1), exp_it=(1, 0))
    values((0, 1))
    values((1, 0))
    staged(exp_it=(1, 1))
    finish(0)
    values((1, 1))
    finish(1)


def _diff(lamv, gain, proj3, proj_lead, *, lambda_init, tq):
    b = proj3.shape[0]
    qb, kb, vb = QB_OFF // LANES, KB_OFF // LANES, VB_OFF // LANES
    kern = functools.partial(_diff_kernel, lambda_init=lambda_init, tq=tq)
    return pl.pallas_call(
        kern,
        grid=(b, DIFF_HEADS, SEQ // tq),
        in_specs=[
            pl.BlockSpec((4, HEAD_DIM), lambda i, h, t: (0, 0)),
            pl.BlockSpec((1, LANES), lambda i, h, t: (0, 0)),
            pl.BlockSpec((None, tq, LANES), lambda i, h, t: (i, t, qb + h)),
            pl.BlockSpec((None, SEQ, LANES), lambda i, h, t: (i, 0, kb + h)),
            pl.BlockSpec((None, SEQ, LANES), lambda i, h, t: (i, 0, vb + h)),
            pl.BlockSpec((BLOCK, LANES), lambda i, h, t: (0, kb + h)),
            pl.BlockSpec((BLOCK, LANES), lambda i, h, t: (0, vb + h)),
        ],
        out_specs=pl.BlockSpec((None, tq, LANES), lambda i, h, t: (i, t, h)),
        out_shape=jax.ShapeDtypeStruct((b, SEQ, DIFF_HEADS * LANES), BF16),
        scratch_shapes=[
            pltpu.VMEM((2, KEYS_VALID, LANES), BF16),
            pltpu.VMEM((DIFF_VROWS, KEYS_EXT), BF16),
            pltpu.VMEM((2, KEYS_VALID, tq), F32),
            pltpu.VMEM((2, KEYS_VALID, tq), F32),
            pltpu.VMEM((KEYS_EXT, tq), BF16),
            pltpu.VMEM((KEYS_EXT, tq), BF16),
        ],
        compiler_params=_cparams(3),
        name="diff_attn",
    )(lamv, gain, proj3, proj3, proj3, proj_lead, proj_lead)


def _merge_kernel(x_ref, os_ref, od_ref, gs_ref, gd_ref, ws_ref, wd_ref, wo_ref, gain_ref, o_ref):
    a = jnp.dot(os_ref[...], ws_ref[...], preferred_element_type=F32)
    b = jnp.dot(od_ref[...], wd_ref[...], preferred_element_type=F32)
    merged = gs_ref[...].astype(F32) * a + gd_ref[...].astype(F32) * b
    y = jnp.dot(merged.astype(BF16), wo_ref[...], preferred_element_type=F32)
    o_ref[...] = x_ref[...] + (y * lax.rsqrt(jnp.mean(y * y, axis=-1, keepdims=True) + RMS_EPS)) * gain_ref[...]


def _merge(x2d, o_swa, o_diff, proj, ws, wd, wo, gain, *, tm):
    n = x2d.shape[0]
    wspec = _resident((D_MODEL, D_MODEL))
    return pl.pallas_call(
        _merge_kernel,
        grid=(n // tm,),
        in_specs=[
            pl.BlockSpec((tm, D_MODEL), lambda i: (i, 0)),
            pl.BlockSpec((tm, D_MODEL), lambda i: (i, 0)),
            pl.BlockSpec((tm, D_MODEL), lambda i: (i, 0)),
            pl.BlockSpec((tm, D_MODEL), lambda i: (i, G_OFF // D_MODEL)),
            pl.BlockSpec((tm, D_MODEL), lambda i: (i, G_OFF // D_MODEL + 1)),
            wspec, wspec, wspec,
            _resident((1, D_MODEL)),
        ],
        out_specs=pl.BlockSpec((tm, D_MODEL), lambda i: (i, 0)),
        out_shape=jax.ShapeDtypeStruct((n, D_MODEL), F32),
        compiler_params=_cparams(1),
        name="merge_out",
    )(x2d, o_swa, o_diff, proj, proj, ws, wd, wo, gain)


FF_CHUNK = 256


def _ffn_kernel(h_ref, g1_ref, wi_ref, wo_ref, g2_ref, o_ref, f_ref):
    h = h_ref[...]
    u = ((h * lax.rsqrt(jnp.mean(h * h, axis=-1, keepdims=True) + RMS_EPS)) * g1_ref[...]).astype(BF16)
    for c in range(D_FF // FF_CHUNK):
        gate = jnp.dot(u, wi_ref[:, c * FF_CHUNK:(c + 1) * FF_CHUNK], preferred_element_type=F32)
        up = jnp.dot(u, wi_ref[:, D_FF + c * FF_CHUNK:D_FF + (c + 1) * FF_CHUNK], preferred_element_type=F32)
        f_ref[:, c * FF_CHUNK:(c + 1) * FF_CHUNK] = (jax.nn.silu(gate) * up).astype(BF16)
    f = jnp.dot(f_ref[...], wo_ref[...], preferred_element_type=F32)
    o_ref[...] = h + (f * lax.rsqrt(jnp.mean(f * f, axis=-1, keepdims=True) + RMS_EPS)) * g2_ref[...]


def _ffn(h2d, g1, wi, wo, g2, *, tm):
    n = h2d.shape[0]
    return pl.pallas_call(
        _ffn_kernel,
        grid=(n // tm,),
        in_specs=[
            pl.BlockSpec((tm, D_MODEL), lambda i: (i, 0)),
            _resident((1, D_MODEL)),
            _resident((D_MODEL, 2 * D_FF)),
            _resident((D_FF, D_MODEL)),
            _resident((1, D_MODEL)),
        ],
        out_specs=pl.BlockSpec((tm, D_MODEL), lambda i: (i, 0)),
        out_shape=jax.ShapeDtypeStruct((n, D_MODEL), F32),
        scratch_shapes=[pltpu.VMEM((tm, D_FF), BF16)],
        compiler_params=_cparams(1),
        name="ffn",
    )(h2d, g1, wi, wo, g2)


def _permute_cols(a):
    qa, ka, va, rest = a[..., :1024], a[..., 1024:1280], a[..., 1280:1536], a[..., 1536:]
    return jnp.concatenate([qa, rest, ka, va], axis=-1)


def kernel(x, meta_tokens, pre_mix_gain, w_in, b_gate, attn_sink, lambda_q1, lambda_k1, lambda_q2, lambda_k2,
           diff_subln_gain, w_branch_swa, w_branch_diff, w_out, post_mix_gain, pre_ffn_gain, w_ffn_in, w_ffn_out,
           post_ffn_gain):
    bsz = x.shape[0]
    if w_in.shape[0] != 1:
        raise NotImplementedError("DEPTH > 1 is not supported")
    l = 0
    lambda_init = 0.8 - 0.6 * math.exp(-0.3 * l)
    row = lambda a: a.reshape(1, -1).astype(F32)

    x2d = x.reshape(bsz * SEQ, D_MODEL)
    lead = jnp.concatenate([jnp.zeros((META_PAD, D_MODEL), x.dtype), meta_tokens.astype(x.dtype)], axis=0)
    real_tables = _rope_tables(jnp.arange(SEQ, dtype=jnp.int32) + N_META)
    lead_tables = _rope_tables(jnp.arange(BLOCK, dtype=jnp.int32) - META_PAD)

    w = _permute_cols(w_in[l]).astype(BF16)
    proj = _inproj(x2d, row(pre_mix_gain[l]), w, row(b_gate[l]), *real_tables, tm=512)
    proj_lead = _inproj(lead, row(pre_mix_gain[l]), w, row(b_gate[l]), *lead_tables, tm=BLOCK)
    proj3 = proj.reshape(bsz, SEQ, IN_COLS)

    o_swa = _swa(row(attn_sink[l]), proj3, proj_lead)
    lamv = jnp.stack([lambda_q1[l], lambda_k1[l], lambda_q2[l], lambda_k2[l]]).astype(F32)
    o_diff = _diff(lamv, row(diff_subln_gain[l]), proj3, proj_lead, lambda_init=lambda_init, tq=1024)

    h2d = _merge(x2d, o_swa.reshape(bsz * SEQ, D_MODEL), o_diff.reshape(bsz * SEQ, D_MODEL), proj,
                 w_branch_swa[l].astype(BF16), w_branch_diff[l].astype(BF16), w_out[l].astype(BF16),
                 row(post_mix_gain[l]), tm=512)
    h2d = _ffn(h2d, row(pre_ffn_gain[l]), w_ffn_in[l].astype(BF16), w_ffn_out[l].astype(BF16),
               row(post_ffn_gain[l]), tm=512)
    return h2d.reshape(bsz, SEQ, D_MODEL)
```

```python
import functools
import math

import jax
import jax.numpy as jnp
from jax import lax
from jax.experimental import pallas as pl
from jax.experimental.pallas import tpu as pltpu

F32 = jnp.float32
BF16 = jnp.bfloat16

D_MODEL = 1024
SEQ = 2048
HEAD_DIM = 64
HALF = HEAD_DIM // 2
BLOCK = 128
N_META = 16
META_PAD = BLOCK - N_META
WINDOW = 128
ROPE_THETA = 10000.0
RMS_EPS = 1e-6
NEG_INF = -1e30
SWA_Q_HEADS = 16
SWA_KV_HEADS = 4
DIFF_HEADS = 8
D_FF = 2816
LANES = 128
F32_ROWS = 8
BF16_ROWS = 16

QA_OFF, QB_OFF, KB_OFF, VB_OFF, G_OFF, KA_OFF, VA_OFF = 0, 1024, 2048, 3072, 4096, 6144, 6400
IN_COLS = 6656
PROJ_GROUP = 512
BAND = 3 * BLOCK
KEYS_EXT = SEQ + BLOCK
KEYS_VALID = SEQ + N_META

LOG2E = math.log2(math.e)
Q_SCALE = HEAD_DIM ** -0.5 * LOG2E

VMEM_LIMIT = 56 * 1024 * 1024


def _cparams(n_axes, flags=None):
    return pltpu.CompilerParams(dimension_semantics=("arbitrary",) * n_axes, vmem_limit_bytes=VMEM_LIMIT,
                                flags=flags)


def _nt_dot(a, b):
    return lax.dot_general(a, b, (((1,), (1,)), ((), ())), preferred_element_type=F32)


def _resident(shape):
    return pl.BlockSpec(shape, lambda *_: (0,) * len(shape), pipeline_mode=pl.Buffered(1))


def _col_kind(col):
    if col < KB_OFF:
        return "rope", Q_SCALE
    if col < VB_OFF:
        return "rope", 1.0
    if col < G_OFF:
        return "plain", 1.0
    if col < KA_OFF:
        return "gate", 1.0
    if col < VA_OFF:
        return "rope", 1.0
    return "plain", 1.0


def _inproj_kernel(x_ref, g_ref, w_ref, b_ref, cos_ref, sa_ref, sb_ref, o_ref):
    xx = x_ref[...]
    ms = jnp.mean(xx * xx, axis=-1, keepdims=True)
    u = ((xx * lax.rsqrt(ms + RMS_EPS)) * g_ref[...]).astype(BF16)
    cos, sa, sb = cos_ref[...], sa_ref[...], sb_ref[...]
    for grp in range(IN_COLS // PROJ_GROUP):
        c0 = grp * PROJ_GROUP
        acc = jnp.dot(u, w_ref[:, c0:c0 + PROJ_GROUP], preferred_element_type=F32)
        for ci in range(PROJ_GROUP // LANES):
            col = c0 + ci * LANES
            t = acc[:, ci * LANES:(ci + 1) * LANES]
            kind, scale = _col_kind(col)
            if kind == "rope":
                t = t * cos + pltpu.roll(t, LANES - HALF, 1) * sa + pltpu.roll(t, HALF, 1) * sb
                if scale != 1.0:
                    t = t * scale
            elif kind == "gate":
                t = jax.nn.sigmoid(t + b_ref[:, col - G_OFF:col - G_OFF + LANES])
            o_ref[:, col:col + LANES] = t.astype(BF16)


def _inproj(x2d, gain, w, b_gate, cos, sa, sb, *, tm):
    n = x2d.shape[0]
    nt = cos.shape[0] // tm
    table = pl.BlockSpec((tm, LANES), lambda i: (i % nt, 0))
    return pl.pallas_call(
        _inproj_kernel,
        grid=(n // tm,),
        in_specs=[
            pl.BlockSpec((tm, D_MODEL), lambda i: (i, 0)),
            _resident((1, D_MODEL)),
            _resident((D_MODEL, IN_COLS)),
            _resident((1, 2 * D_MODEL)),
            table, table, table,
        ],
        out_specs=pl.BlockSpec((tm, IN_COLS), lambda i: (i, 0)),
        out_shape=jax.ShapeDtypeStruct((n, IN_COLS), BF16),
        compiler_params=_cparams(1),
        name="inproj",
    )(x2d, gain, w, b_gate, cos, sa, sb)


def _rope_tables(pos):
    inv_freq = ROPE_THETA ** (-jnp.arange(0, HEAD_DIM, 2, dtype=F32) / HEAD_DIM)
    ang = pos.astype(F32)[:, None] * inv_freq[None, :]
    lane = jnp.arange(LANES)
    cos = jnp.cos(ang)[:, lane % HALF]
    sin = jnp.sin(ang)[:, lane % HALF]
    first = (lane % HEAD_DIM) < HALF
    return cos, jnp.where(first, -sin, 0.0), jnp.where(first, 0.0, sin)


SWA_VROWS = HEAD_DIM + BF16_ROWS
SWA_KROWS = BAND + N_META
SWA_UNROLL = 4
SWA_VALUE_LAG = 5


def _swa_kernel(sink_ref, q_ref, k_ref, v_ref, km_ref, vm_ref, o_ref, kx_ref, kmx_ref, vt_ref):
    lo = lax.broadcasted_iota(jnp.int32, (1, LANES), 1) < HEAD_DIM

    def expand_k(src, dst, sl_src, sl_dst):
        for lb in range(2):
            blk = src[sl_src, lb * LANES:(lb + 1) * LANES].astype(F32)
            rolled = pltpu.roll(blk, HEAD_DIM, 1)
            g0, g1 = 2 * lb, 2 * lb + 1
            dst[2 * g0, sl_dst, :] = jnp.where(lo, blk, 0.0).astype(BF16)
            dst[2 * g0 + 1, sl_dst, :] = jnp.where(lo, 0.0, rolled).astype(BF16)
            dst[2 * g1, sl_dst, :] = jnp.where(lo, rolled, 0.0).astype(BF16)
            dst[2 * g1 + 1, sl_dst, :] = jnp.where(lo, 0.0, blk).astype(BF16)

    def put_vt(src, sl_src, sl_dst):
        vt = src[sl_src, :].astype(F32).T
        for g in range(SWA_KV_HEADS):
            vt_ref[g, 0:HEAD_DIM, sl_dst] = vt[g * HEAD_DIM:(g + 1) * HEAD_DIM].astype(BF16)

    def prep_body(c, carry):
        sl = pl.ds(pl.multiple_of(c * 256, 256), 256)
        expand_k(k_ref, kx_ref, sl, sl)
        put_vt(v_ref, sl, sl)
        return carry
    lax.fori_loop(0, SEQ // 256, prep_body, 0)
    expand_k(km_ref, kmx_ref, slice(META_PAD, BLOCK), slice(None))
    put_vt(vm_ref, slice(None), slice(SEQ, KEYS_EXT))
    vt_ref[:, HEAD_DIM:SWA_VROWS, :] = jnp.ones((SWA_KV_HEADS, BF16_ROWS, KEYS_EXT), BF16)

    krow = lax.broadcasted_iota(jnp.int32, (SWA_KROWS, 2 * LANES), 0)
    kq_delta = krow - lax.broadcasted_iota(jnp.int32, (SWA_KROWS, 2 * LANES), 1) % LANES
    first_blk = lax.broadcasted_iota(jnp.int32, (1, 2 * LANES), 1) < LANES

    filler_rows = jnp.zeros((META_PAD, 2 * LANES), BF16)

    def body(j, carry):
        q0, start, bias = [], [], []
        for i in range(SWA_UNROLL):
            r = j * SWA_UNROLL + i
            q0.append(pl.multiple_of(r * BLOCK, BLOCK))
            start.append(pl.multiple_of(jnp.clip(r * BLOCK - BLOCK, 0, SEQ - BAND), BLOCK))
            visible = (krow >= BAND) | (jnp.abs(kq_delta + (start[i] - q0[i])) <= WINDOW)
            bias.append(jnp.where(visible, 0.0, NEG_INF).astype(F32))

        chains = [(i, g, v) for i in range(SWA_UNROLL) for g in range(SWA_KV_HEADS) for v in range(2)]

        def scores(i, g, v):
            lb0, lb1 = 2 * g, 2 * g + 1
            qq = jnp.concatenate([q_ref[pl.ds(q0[i], BLOCK), lb0 * LANES:(lb0 + 1) * LANES],
                                  q_ref[pl.ds(q0[i], BLOCK), lb1 * LANES:(lb1 + 1) * LANES]], axis=0)
            kcat = jnp.concatenate([kx_ref[2 * g + v, pl.ds(start[i], BAND), :], kmx_ref[2 * g + v]], axis=0)
            sink = jnp.where(first_blk, sink_ref[0, 2 * lb0 + v], sink_ref[0, 2 * lb1 + v]) * LOG2E
            return _nt_dot(kcat, qq), sink, i

        def masked_max(s, sink, i):
            s = s + bias[i]
            return s, jnp.maximum(jnp.max(s, axis=0, keepdims=True), sink), sink

        def probs(s, m, sink):
            p = jnp.exp2(s - m).astype(BF16)
            return jnp.concatenate([p[0:BAND], filler_rows, p[BAND:SWA_KROWS]], axis=0), jnp.exp2(sink - m)

        def values(i, g, pt, psink):
            vcat = jnp.concatenate([vt_ref[g, :, pl.ds(start[i], BAND)], vt_ref[g, :, SEQ:KEYS_EXT]], axis=1)
            o = jnp.dot(vcat, pt, preferred_element_type=F32)
            return o[0:HEAD_DIM] * (1.0 / (o[HEAD_DIM:HEAD_DIM + 1] + psink))

        st_a, st_b, st_c, st_d = {}, {}, {}, {}
        for step in range(len(chains) + SWA_VALUE_LAG):
            if step < len(chains):
                st_a[step] = scores(*chains[step])
            if 0 <= step - 1 < len(chains):
                st_b[step - 1] = masked_max(*st_a.pop(step - 1))
            if 0 <= step - 2 < len(chains):
                st_c[step - 2] = probs(*st_b.pop(step - 2))
            if 0 <= step - SWA_VALUE_LAG < len(chains):
                i, g, v = chains[step - SWA_VALUE_LAG]
                st_d[v] = values(i, g, *st_c.pop(step - SWA_VALUE_LAG))
                if v == 1:
                    ot = jnp.concatenate([st_d.pop(0), st_d.pop(1)], axis=0)
                    for half, lb in enumerate((2 * g, 2 * g + 1)):
                        o_ref[pl.ds(q0[i], BLOCK), lb * LANES:(lb + 1) * LANES] = (
                            ot[:, half * LANES:(half + 1) * LANES].T.astype(BF16))
        return carry
    lax.fori_loop(0, SEQ // (BLOCK * SWA_UNROLL), body, 0)


def _swa(sink, proj3, proj_lead):
    b = proj3.shape[0]
    return pl.pallas_call(
        _swa_kernel,
        grid=(b,),
        in_specs=[
            pl.BlockSpec(memory_space=pltpu.SMEM),
            pl.BlockSpec((None, SEQ, 1024), lambda i: (i, 0, QA_OFF // 1024)),
            pl.BlockSpec((None, SEQ, 256), lambda i: (i, 0, KA_OFF // 256)),
            pl.BlockSpec((None, SEQ, 256), lambda i: (i, 0, VA_OFF // 256)),
            pl.BlockSpec((BLOCK, 256), lambda i: (0, KA_OFF // 256)),
            pl.BlockSpec((BLOCK, 256), lambda i: (0, VA_OFF // 256)),
        ],
        out_specs=pl.BlockSpec((None, SEQ, 1024), lambda i: (i, 0, 0)),
        out_shape=jax.ShapeDtypeStruct((b, SEQ, 1024), BF16),
        scratch_shapes=[
            pltpu.VMEM((2 * SWA_KV_HEADS, SEQ, LANES), BF16),
            pltpu.VMEM((2 * SWA_KV_HEADS, N_META, LANES), BF16),
            pltpu.VMEM((SWA_KV_HEADS, SWA_VROWS, KEYS_EXT), BF16),
        ],
        compiler_params=_cparams(1),
        name="swa_attn",
    )(sink, proj3, proj3, proj3, proj_lead, proj_lead)


DIFF_VROWS = 2 * HEAD_DIM + BF16_ROWS
DIFF_KCHUNK = 512
DIFF_PART = 512


def _diff_kernel(lamv_ref, gain_ref, q_ref, k_ref, v_ref, km_ref, vm_ref, o_ref, kx_ref, vt_ref,
                 s0_ref, s1_ref, e0_ref, e1_ref, *, lambda_init, tq):
    s_refs, e_refs = (s0_ref, s1_ref), (e0_ref, e1_ref)

    @pl.when(pl.program_id(2) == 0)
    def _():
        lo = lax.broadcasted_iota(jnp.int32, (1, LANES), 1) < HEAD_DIM
        zero = jnp.zeros((), BF16)

        def put_k(kk, sl):
            kx_ref[0, sl, :] = jnp.where(lo, kk, zero)
            kx_ref[1, sl, :] = jnp.where(lo, zero, kk)
        for c in range(SEQ // 512):
            put_k(k_ref[c * 512:(c + 1) * 512, :], slice(c * 512, (c + 1) * 512))
            for cc in range(2):
                sl = slice(c * 512 + cc * 256, c * 512 + (cc + 1) * 256)
                vt_ref[0:2 * HEAD_DIM, sl] = v_ref[sl, :].astype(F32).T.astype(BF16)
        put_k(km_ref[META_PAD:BLOCK, :], slice(SEQ, KEYS_VALID))
        vt_ref[0:2 * HEAD_DIM, SEQ:KEYS_EXT] = vm_ref[...].astype(F32).T.astype(BF16)
        vt_ref[2 * HEAD_DIM:DIFF_VROWS, :] = jnp.ones((BF16_ROWS, KEYS_EXT), BF16)
        for e_ref in e_refs:
            e_ref[SEQ:SEQ + META_PAD, :] = jnp.zeros((META_PAD, 2 * DIFF_PART), BF16)

    lv = lamv_ref[...]
    lam = (jnp.exp(jnp.sum(lv[0:1] * lv[1:2], axis=-1, keepdims=True))
           - jnp.exp(jnp.sum(lv[2:3] * lv[3:4], axis=-1, keepdims=True)) + lambda_init)
    chunks = [(k0, DIFF_KCHUNK if k0 + DIFF_KCHUNK < SEQ else KEYS_VALID - k0) for k0 in range(0, SEQ, DIFF_KCHUNK)]
    qrows = [slice(p * DIFF_PART, (p + 1) * DIFF_PART) for p in range(tq // DIFF_PART)]
    qcols = [slice((p % 2) * DIFF_PART, (p % 2 + 1) * DIFF_PART) for p in range(tq // DIFF_PART)]

    t = pl.program_id(1) * pl.num_programs(2) + pl.program_id(2)
    half_st = t & 1
    half_ld = t - ((t >> 1) << 1)

    acc, mx, out = {}, {}, {}

    def scores(it, k0, kn):
        qh, mp = it
        s = _nt_dot(kx_ref[mp, k0:k0 + kn, :], q_ref[qrows[qh], :])
        s_refs[mp][half_st, k0:k0 + kn, qcols[qh]] = s
        a = acc.setdefault(it, [None, None])
        for i, r in enumerate(range(0, kn, F32_ROWS)):
            x = s[r:r + F32_ROWS, :]
            a[i % 2] = x if a[i % 2] is None else jnp.maximum(a[i % 2], x)

    def col_max(it):
        m8 = jnp.maximum(*acc.pop(it))
        mx[it] = jnp.broadcast_to(jnp.max(m8, axis=0, keepdims=True), (BF16_ROWS, m8.shape[1]))

    def exps(it, k0, kn):
        qh, mp = it
        for r in range(k0, k0 + kn, BF16_ROWS):
            x = s_refs[mp][half_ld, r:r + BF16_ROWS, qcols[qh]]
            er = r if r < SEQ else r + META_PAD
            e_refs[mp][er:er + BF16_ROWS, qcols[qh]] = jnp.exp2(x - mx[it]).astype(BF16)

    def values(it):
        qh, mp = it
        out[it] = jnp.dot(vt_ref[...], e_refs[mp][:, qcols[qh]], preferred_element_type=F32)

    def finish(qh):
        d = 2 * HEAD_DIM
        o1, o2 = out.pop((qh, 0)), out.pop((qh, 1))
        r1 = 1.0 / o1[d:d + 1]
        r2 = lam / o2[d:d + 1]
        o = (o1[0:d] * r1 - o2[0:d] * r2).T
        y = o * lax.rsqrt(jnp.mean(o * o, axis=-1, keepdims=True) + RMS_EPS) * gain_ref[...]
        o_ref[qrows[qh], :] = (y * (1.0 - lambda_init)).astype(BF16)

    def staged(score_it=None, exp_it=None):
        if exp_it is not None:
            col_max(exp_it)
        for ch in chunks:
            if score_it is not None:
                scores(score_it, *ch)
            if exp_it is not None:
                exps(exp_it, *ch)

    items = [(p, mp) for p in range(len(qrows)) for mp in range(2)]
    for n in range(len(items) + 1):
        staged(score_it=items[n] if n < len(items) else None, exp_it=items[n - 1] if n > 0 else None)
        if n > 0:
            values(items[n - 1])
            if items[n - 1][1] == 1:
                finish(items[n - 1][0])


def _diff(lamv, gain, proj3, proj_lead, *, lambda_init, tq):
    b = proj3.shape[0]
    qb, kb, vb = QB_OFF // LANES, KB_OFF // LANES, VB_OFF // LANES
    kern = functools.partial(_diff_kernel, lambda_init=lambda_init, tq=tq)
    return pl.pallas_call(
        kern,
        grid=(b, DIFF_HEADS, SEQ // tq),
        in_specs=[
            pl.BlockSpec((4, HEAD_DIM), lambda i, h, t: (0, 0)),
            pl.BlockSpec((1, LANES), lambda i, h, t: (0, 0)),
            pl.BlockSpec((None, tq, LANES), lambda i, h, t: (i, t, qb + h)),
            pl.BlockSpec((None, SEQ, LANES), lambda i, h, t: (i, 0, kb + h)),
            pl.BlockSpec((None, SEQ, LANES), lambda i, h, t: (i, 0, vb + h)),
            pl.BlockSpec((BLOCK, LANES), lambda i, h, t: (0, kb + h)),
            pl.BlockSpec((BLOCK, LANES), lambda i, h, t: (0, vb + h)),
        ],
        out_specs=pl.BlockSpec((None, tq, LANES), lambda i, h, t: (i, t, h)),
        out_shape=jax.ShapeDtypeStruct((b, SEQ, DIFF_HEADS * LANES), BF16),
        scratch_shapes=[
            pltpu.VMEM((2, KEYS_VALID, LANES), BF16),
            pltpu.VMEM((DIFF_VROWS, KEYS_EXT), BF16),
            pltpu.VMEM((2, KEYS_VALID, 2 * DIFF_PART), F32),
            pltpu.VMEM((2, KEYS_VALID, 2 * DIFF_PART), F32),
            pltpu.VMEM((KEYS_EXT, 2 * DIFF_PART), BF16),
            pltpu.VMEM((KEYS_EXT, 2 * DIFF_PART), BF16),
        ],
        compiler_params=_cparams(3),
        name="diff_attn",
    )(lamv, gain, proj3, proj3, proj3, proj_lead, proj_lead)


def _merge_kernel(x_ref, os_ref, od_ref, gs_ref, gd_ref, ws_ref, wd_ref, wo_ref, gain_ref, o_ref):
    a = jnp.dot(os_ref[...], ws_ref[...], preferred_element_type=F32)
    b = jnp.dot(od_ref[...], wd_ref[...], preferred_element_type=F32)
    merged = gs_ref[...].astype(F32) * a + gd_ref[...].astype(F32) * b
    y = jnp.dot(merged.astype(BF16), wo_ref[...], preferred_element_type=F32)
    o_ref[...] = x_ref[...] + (y * lax.rsqrt(jnp.mean(y * y, axis=-1, keepdims=True) + RMS_EPS)) * gain_ref[...]


def _merge(x2d, o_swa, o_diff, proj, ws, wd, wo, gain, *, tm):
    n = x2d.shape[0]
    wspec = _resident((D_MODEL, D_MODEL))
    return pl.pallas_call(
        _merge_kernel,
        grid=(n // tm,),
        in_specs=[
            pl.BlockSpec((tm, D_MODEL), lambda i: (i, 0)),
            pl.BlockSpec((tm, D_MODEL), lambda i: (i, 0)),
            pl.BlockSpec((tm, D_MODEL), lambda i: (i, 0)),
            pl.BlockSpec((tm, D_MODEL), lambda i: (i, G_OFF // D_MODEL)),
            pl.BlockSpec((tm, D_MODEL), lambda i: (i, G_OFF // D_MODEL + 1)),
            wspec, wspec, wspec,
            _resident((1, D_MODEL)),
        ],
        out_specs=pl.BlockSpec((tm, D_MODEL), lambda i: (i, 0)),
        out_shape=jax.ShapeDtypeStruct((n, D_MODEL), F32),
        compiler_params=_cparams(1),
        name="merge_out",
    )(x2d, o_swa, o_diff, proj, proj, ws, wd, wo, gain)


FF_CHUNK = 256


def _ffn_kernel(h_ref, g1_ref, wi_ref, wo_ref, g2_ref, o_ref, f_ref):
    h = h_ref[...]
    u = ((h * lax.rsqrt(jnp.mean(h * h, axis=-1, keepdims=True) + RMS_EPS)) * g1_ref[...]).astype(BF16)
    for c in range(D_FF // FF_CHUNK):
        gate = jnp.dot(u, wi_ref[:, c * FF_CHUNK:(c + 1) * FF_CHUNK], preferred_element_type=F32)
        up = jnp.dot(u, wi_ref[:, D_FF + c * FF_CHUNK:D_FF + (c + 1) * FF_CHUNK], preferred_element_type=F32)
        f_ref[:, c * FF_CHUNK:(c + 1) * FF_CHUNK] = (jax.nn.silu(gate) * up).astype(BF16)
    f = jnp.dot(f_ref[...], wo_ref[...], preferred_element_type=F32)
    o_ref[...] = h + (f * lax.rsqrt(jnp.mean(f * f, axis=-1, keepdims=True) + RMS_EPS)) * g2_ref[...]


def _ffn(h2d, g1, wi, wo, g2, *, tm):
    n = h2d.shape[0]
    return pl.pallas_call(
        _ffn_kernel,
        grid=(n // tm,),
        in_specs=[
            pl.BlockSpec((tm, D_MODEL), lambda i: (i, 0)),
            _resident((1, D_MODEL)),
            _resident((D_MODEL, 2 * D_FF)),
            _resident((D_FF, D_MODEL)),
            _resident((1, D_MODEL)),
        ],
        out_specs=pl.BlockSpec((tm, D_MODEL), lambda i: (i, 0)),
        out_shape=jax.ShapeDtypeStruct((n, D_MODEL), F32),
        scratch_shapes=[pltpu.VMEM((tm, D_FF), BF16)],
        compiler_params=_cparams(1),
        name="ffn",
    )(h2d, g1, wi, wo, g2)


def _permute_cols(a):
    qa, ka, va, rest = a[..., :1024], a[..., 1024:1280], a[..., 1280:1536], a[..., 1536:]
    return jnp.concatenate([qa, rest, ka, va], axis=-1)


def kernel(x, meta_tokens, pre_mix_gain, w_in, b_gate, attn_sink, lambda_q1, lambda_k1, lambda_q2, lambda_k2,
           diff_subln_gain, w_branch_swa, w_branch_diff, w_out, post_mix_gain, pre_ffn_gain, w_ffn_in, w_ffn_out,
           post_ffn_gain):
    bsz = x.shape[0]
    if w_in.shape[0] != 1:
        raise NotImplementedError("DEPTH > 1 is not supported")
    l = 0
    lambda_init = 0.8 - 0.6 * math.exp(-0.3 * l)
    row = lambda a: a.reshape(1, -1).astype(F32)

    x2d = x.reshape(bsz * SEQ, D_MODEL)
    lead = jnp.concatenate([jnp.zeros((META_PAD, D_MODEL), x.dtype), meta_tokens.astype(x.dtype)], axis=0)
    real_tables = _rope_tables(jnp.arange(SEQ, dtype=jnp.int32) + N_META)
    lead_tables = _rope_tables(jnp.arange(BLOCK, dtype=jnp.int32) - META_PAD)

    w = _permute_cols(w_in[l]).astype(BF16)
    proj = _inproj(x2d, row(pre_mix_gain[l]), w, row(b_gate[l]), *real_tables, tm=512)
    proj_lead = _inproj(lead, row(pre_mix_gain[l]), w, row(b_gate[l]), *lead_tables, tm=BLOCK)
    proj3 = proj.reshape(bsz, SEQ, IN_COLS)

    o_swa = _swa(row(attn_sink[l]), proj3, proj_lead)
    lamv = jnp.stack([lambda_q1[l], lambda_k1[l], lambda_q2[l], lambda_k2[l]]).astype(F32)
    o_diff = _diff(lamv, row(diff_subln_gain[l]), proj3, proj_lead, lambda_init=lambda_init, tq=2048)

    h2d = _merge(x2d, o_swa.reshape(bsz * SEQ, D_MODEL), o_diff.reshape(bsz * SEQ, D_MODEL), proj,
                 w_branch_swa[l].astype(BF16), w_branch_diff[l].astype(BF16), w_out[l].astype(BF16),
                 row(post_mix_gain[l]), tm=512)
    h2d = _ffn(h2d, row(pre_ffn_gain[l]), w_ffn_in[l].astype(BF16), w_ffn_out[l].astype(BF16),
               row(post_ffn_gain[l]), tm=512)
    return h2d.reshape(bsz, SEQ, D_MODEL)
```

```python
import functools
import math

import jax
import jax.numpy as jnp
from jax import lax
from jax.experimental import pallas as pl
from jax.experimental.pallas import tpu as pltpu

F32 = jnp.float32
BF16 = jnp.bfloat16

D_MODEL = 1024
SEQ = 2048
HEAD_DIM = 64
HALF = HEAD_DIM // 2
BLOCK = 128
N_META = 16
META_PAD = BLOCK - N_META
WINDOW = 128
ROPE_THETA = 10000.0
RMS_EPS = 1e-6
NEG_INF = -1e30
SWA_Q_HEADS = 16
SWA_KV_HEADS = 4
DIFF_HEADS = 8
D_FF = 2816
LANES = 128
F32_ROWS = 8
BF16_ROWS = 16

QA_OFF, QB_OFF, KB_OFF, VB_OFF, G_OFF, KA_OFF, VA_OFF = 0, 1024, 2048, 3072, 4096, 6144, 6400
IN_COLS = 6656
PROJ_GROUP = 512
BAND = 3 * BLOCK
KEYS_EXT = SEQ + BLOCK
KEYS_VALID = SEQ + N_META

LOG2E = math.log2(math.e)
Q_SCALE = HEAD_DIM ** -0.5 * LOG2E

VMEM_LIMIT = 56 * 1024 * 1024


def _cparams(n_axes, flags=None):
    return pltpu.CompilerParams(dimension_semantics=("arbitrary",) * n_axes, vmem_limit_bytes=VMEM_LIMIT,
                                flags=flags)


def _nt_dot(a, b):
    return lax.dot_general(a, b, (((1,), (1,)), ((), ())), preferred_element_type=F32)


def _resident(shape):
    return pl.BlockSpec(shape, lambda *_: (0,) * len(shape), pipeline_mode=pl.Buffered(1))


def _col_kind(col):
    if col < KB_OFF:
        return "rope", Q_SCALE
    if col < VB_OFF:
        return "rope", 1.0
    if col < G_OFF:
        return "plain", 1.0
    if col < KA_OFF:
        return "gate", 1.0
    if col < VA_OFF:
        return "rope", 1.0
    return "plain", 1.0


def _inproj_kernel(x_ref, g_ref, w_ref, b_ref, cos_ref, sa_ref, sb_ref, o_ref):
    xx = x_ref[...]
    ms = jnp.mean(xx * xx, axis=-1, keepdims=True)
    u = ((xx * lax.rsqrt(ms + RMS_EPS)) * g_ref[...]).astype(BF16)
    cos, sa, sb = cos_ref[...], sa_ref[...], sb_ref[...]
    for grp in range(IN_COLS // PROJ_GROUP):
        c0 = grp * PROJ_GROUP
        acc = jnp.dot(u, w_ref[:, c0:c0 + PROJ_GROUP], preferred_element_type=F32)
        for ci in range(PROJ_GROUP // LANES):
            col = c0 + ci * LANES
            t = acc[:, ci * LANES:(ci + 1) * LANES]
            kind, scale = _col_kind(col)
            if kind == "rope":
                t = t * cos + pltpu.roll(t, LANES - HALF, 1) * sa + pltpu.roll(t, HALF, 1) * sb
                if scale != 1.0:
                    t = t * scale
            elif kind == "gate":
                t = jax.nn.sigmoid(t + b_ref[:, col - G_OFF:col - G_OFF + LANES])
            o_ref[:, col:col + LANES] = t.astype(BF16)


def _inproj(x2d, gain, w, b_gate, cos, sa, sb, *, tm):
    n = x2d.shape[0]
    nt = cos.shape[0] // tm
    table = pl.BlockSpec((tm, LANES), lambda i: (i % nt, 0))
    return pl.pallas_call(
        _inproj_kernel,
        grid=(n // tm,),
        in_specs=[
            pl.BlockSpec((tm, D_MODEL), lambda i: (i, 0)),
            _resident((1, D_MODEL)),
            _resident((D_MODEL, IN_COLS)),
            _resident((1, 2 * D_MODEL)),
            table, table, table,
        ],
        out_specs=pl.BlockSpec((tm, IN_COLS), lambda i: (i, 0)),
        out_shape=jax.ShapeDtypeStruct((n, IN_COLS), BF16),
        compiler_params=_cparams(1),
        name="inproj",
    )(x2d, gain, w, b_gate, cos, sa, sb)


def _rope_tables(pos):
    inv_freq = ROPE_THETA ** (-jnp.arange(0, HEAD_DIM, 2, dtype=F32) / HEAD_DIM)
    ang = pos.astype(F32)[:, None] * inv_freq[None, :]
    lane = jnp.arange(LANES)
    cos = jnp.cos(ang)[:, lane % HALF]
    sin = jnp.sin(ang)[:, lane % HALF]
    first = (lane % HEAD_DIM) < HALF
    return cos, jnp.where(first, -sin, 0.0), jnp.where(first, 0.0, sin)


SWA_VROWS = HEAD_DIM + BF16_ROWS
SWA_KROWS = BAND + N_META
SWA_UNROLL = 4
SWA_VALUE_LAG = 5


def _swa_kernel(sink_ref, q_ref, k_ref, v_ref, km_ref, vm_ref, o_ref, kx_ref, kmx_ref, vt_ref):
    lo = lax.broadcasted_iota(jnp.int32, (1, LANES), 1) < HEAD_DIM

    def expand_k(src, dst, sl_src, sl_dst):
        for lb in range(2):
            blk = src[sl_src, lb * LANES:(lb + 1) * LANES].astype(F32)
            rolled = pltpu.roll(blk, HEAD_DIM, 1)
            g0, g1 = 2 * lb, 2 * lb + 1
            dst[2 * g0, sl_dst, :] = jnp.where(lo, blk, 0.0).astype(BF16)
            dst[2 * g0 + 1, sl_dst, :] = jnp.where(lo, 0.0, rolled).astype(BF16)
            dst[2 * g1, sl_dst, :] = jnp.where(lo, rolled, 0.0).astype(BF16)
            dst[2 * g1 + 1, sl_dst, :] = jnp.where(lo, 0.0, blk).astype(BF16)

    def put_vt(src, sl_src, sl_dst):
        vt = src[sl_src, :].astype(F32).T
        for g in range(SWA_KV_HEADS):
            vt_ref[g, 0:HEAD_DIM, sl_dst] = vt[g * HEAD_DIM:(g + 1) * HEAD_DIM].astype(BF16)

    def prep_body(c, carry):
        sl = pl.ds(pl.multiple_of(c * 256, 256), 256)
        expand_k(k_ref, kx_ref, sl, sl)
        put_vt(v_ref, sl, sl)
        return carry
    lax.fori_loop(0, SEQ // 256, prep_body, 0)
    expand_k(km_ref, kmx_ref, slice(META_PAD, BLOCK), slice(None))
    put_vt(vm_ref, slice(None), slice(SEQ, KEYS_EXT))
    vt_ref[:, HEAD_DIM:SWA_VROWS, :] = jnp.ones((SWA_KV_HEADS, BF16_ROWS, KEYS_EXT), BF16)

    krow = lax.broadcasted_iota(jnp.int32, (SWA_KROWS, 2 * LANES), 0)
    kq_delta = krow - lax.broadcasted_iota(jnp.int32, (SWA_KROWS, 2 * LANES), 1) % LANES
    first_blk = lax.broadcasted_iota(jnp.int32, (1, 2 * LANES), 1) < LANES

    filler_rows = jnp.zeros((META_PAD, 2 * LANES), BF16)

    def body(j, carry):
        q0, start, bias = [], [], []
        for i in range(SWA_UNROLL):
            r = j * SWA_UNROLL + i
            q0.append(pl.multiple_of(r * BLOCK, BLOCK))
            start.append(pl.multiple_of(jnp.clip(r * BLOCK - BLOCK, 0, SEQ - BAND), BLOCK))
            visible = (krow >= BAND) | (jnp.abs(kq_delta + (start[i] - q0[i])) <= WINDOW)
            bias.append(jnp.where(visible, 0.0, NEG_INF).astype(F32))

        chains = [(i, g, v) for i in range(SWA_UNROLL) for g in range(SWA_KV_HEADS) for v in range(2)]

        def scores(i, g, v):
            lb0, lb1 = 2 * g, 2 * g + 1
            qq = jnp.concatenate([q_ref[pl.ds(q0[i], BLOCK), lb0 * LANES:(lb0 + 1) * LANES],
                                  q_ref[pl.ds(q0[i], BLOCK), lb1 * LANES:(lb1 + 1) * LANES]], axis=0)
            kcat = jnp.concatenate([kx_ref[2 * g + v, pl.ds(start[i], BAND), :], kmx_ref[2 * g + v]], axis=0)
            sink = jnp.where(first_blk, sink_ref[0, 2 * lb0 + v], sink_ref[0, 2 * lb1 + v]) * LOG2E
            return _nt_dot(kcat, qq), sink, i

        def masked_max(s, sink, i):
            s = s + bias[i]
            return s, jnp.maximum(jnp.max(s, axis=0, keepdims=True), sink), sink

        def probs(s, m, sink):
            p = jnp.exp2(s - m).astype(BF16)
            return jnp.concatenate([p[0:BAND], filler_rows, p[BAND:SWA_KROWS]], axis=0), jnp.exp2(sink - m)

        def values(i, g, pt, psink):
            vcat = jnp.concatenate([vt_ref[g, :, pl.ds(start[i], BAND)], vt_ref[g, :, SEQ:KEYS_EXT]], axis=1)
            o = jnp.dot(vcat, pt, preferred_element_type=F32)
            return o[0:HEAD_DIM] * (1.0 / (o[HEAD_DIM:HEAD_DIM + 1] + psink))

        st_a, st_b, st_c, st_d = {}, {}, {}, {}
        for step in range(len(chains) + SWA_VALUE_LAG):
            if step < len(chains):
                st_a[step] = scores(*chains[step])
            if 0 <= step - 1 < len(chains):
                st_b[step - 1] = masked_max(*st_a.pop(step - 1))
            if 0 <= step - 2 < len(chains):
                st_c[step - 2] = probs(*st_b.pop(step - 2))
            if 0 <= step - SWA_VALUE_LAG < len(chains):
                i, g, v = chains[step - SWA_VALUE_LAG]
                st_d[v] = values(i, g, *st_c.pop(step - SWA_VALUE_LAG))
                if v == 1:
                    ot = jnp.concatenate([st_d.pop(0), st_d.pop(1)], axis=0)
                    for half, lb in enumerate((2 * g, 2 * g + 1)):
                        o_ref[pl.ds(q0[i], BLOCK), lb * LANES:(lb + 1) * LANES] = (
                            ot[:, half * LANES:(half + 1) * LANES].T.astype(BF16))
        return carry
    lax.fori_loop(0, SEQ // (BLOCK * SWA_UNROLL), body, 0)


def _swa(sink, proj3, proj_lead):
    b = proj3.shape[0]
    return pl.pallas_call(
        _swa_kernel,
        grid=(b,),
        in_specs=[
            pl.BlockSpec(memory_space=pltpu.SMEM),
            pl.BlockSpec((None, SEQ, 1024), lambda i: (i, 0, QA_OFF // 1024)),
            pl.BlockSpec((None, SEQ, 256), lambda i: (i, 0, KA_OFF // 256)),
            pl.BlockSpec((None, SEQ, 256), lambda i: (i, 0, VA_OFF // 256)),
            pl.BlockSpec((BLOCK, 256), lambda i: (0, KA_OFF // 256)),
            pl.BlockSpec((BLOCK, 256), lambda i: (0, VA_OFF // 256)),
        ],
        out_specs=pl.BlockSpec((None, SEQ, 1024), lambda i: (i, 0, 0)),
        out_shape=jax.ShapeDtypeStruct((b, SEQ, 1024), BF16),
        scratch_shapes=[
            pltpu.VMEM((2 * SWA_KV_HEADS, SEQ, LANES), BF16),
            pltpu.VMEM((2 * SWA_KV_HEADS, N_META, LANES), BF16),
            pltpu.VMEM((SWA_KV_HEADS, SWA_VROWS, KEYS_EXT), BF16),
        ],
        compiler_params=_cparams(1),
        name="swa_attn",
    )(sink, proj3, proj3, proj3, proj_lead, proj_lead)


DIFF_VROWS = 2 * HEAD_DIM + BF16_ROWS
DIFF_KCHUNK = 512
DIFF_PART = 512


def _diff_kernel(lamv_ref, gain_ref, q_ref, k_ref, v_ref, km_ref, vm_ref, o_ref, kx_ref, vt_ref,
                 s0_ref, s1_ref, e0_ref, e1_ref, *, lambda_init, tq):
    s_refs, e_refs = (s0_ref, s1_ref), (e0_ref, e1_ref)

    @pl.when(pl.program_id(2) == 0)
    def _():
        lo = lax.broadcasted_iota(jnp.int32, (1, LANES), 1) < HEAD_DIM
        zero = jnp.zeros((), BF16)

        def put_k(kk, sl):
            kx_ref[0, sl, :] = jnp.where(lo, kk, zero)
            kx_ref[1, sl, :] = jnp.where(lo, zero, kk)
        for c in range(SEQ // 512):
            put_k(k_ref[c * 512:(c + 1) * 512, :], slice(c * 512, (c + 1) * 512))
            for cc in range(2):
                sl = slice(c * 512 + cc * 256, c * 512 + (cc + 1) * 256)
                vt_ref[0:2 * HEAD_DIM, sl] = v_ref[sl, :].astype(F32).T.astype(BF16)
        put_k(km_ref[META_PAD:BLOCK, :], slice(SEQ, KEYS_VALID))
        vt_ref[0:2 * HEAD_DIM, SEQ:KEYS_EXT] = vm_ref[...].astype(F32).T.astype(BF16)
        vt_ref[2 * HEAD_DIM:DIFF_VROWS, :] = jnp.ones((BF16_ROWS, KEYS_EXT), BF16)
        for e_ref in e_refs:
            e_ref[SEQ:SEQ + META_PAD, :] = jnp.zeros((META_PAD, 2 * DIFF_PART), BF16)

    lv = lamv_ref[...]
    lam = (jnp.exp(jnp.sum(lv[0:1] * lv[1:2], axis=-1, keepdims=True))
           - jnp.exp(jnp.sum(lv[2:3] * lv[3:4], axis=-1, keepdims=True)) + lambda_init)
    chunks = [(k0, DIFF_KCHUNK if k0 + DIFF_KCHUNK < SEQ else KEYS_VALID - k0) for k0 in range(0, SEQ, DIFF_KCHUNK)]
    qrows = [slice(p * DIFF_PART, (p + 1) * DIFF_PART) for p in range(tq // DIFF_PART)]
    qcols = [slice((p % 2) * DIFF_PART, (p % 2 + 1) * DIFF_PART) for p in range(tq // DIFF_PART)]

    t = pl.program_id(1) * pl.num_programs(2) + pl.program_id(2)
    half_st = t & 1
    half_ld = t - ((t >> 1) << 1)

    acc, mx, out = {}, {}, {}

    def scores(it, k0, kn):
        qh, mp = it
        s = _nt_dot(kx_ref[mp, k0:k0 + kn, :], q_ref[qrows[qh], :])
        s_refs[mp][half_st, k0:k0 + kn, qcols[qh]] = s
        a = acc.setdefault(it, [None, None])
        for i, r in enumerate(range(0, kn, F32_ROWS)):
            x = s[r:r + F32_ROWS, :]
            a[i % 2] = x if a[i % 2] is None else jnp.maximum(a[i % 2], x)

    def col_max(it):
        m8 = jnp.maximum(*acc.pop(it))
        mx[it] = jnp.broadcast_to(jnp.max(m8, axis=0, keepdims=True), (BF16_ROWS, m8.shape[1]))

    def exps(it, k0, kn):
        qh, mp = it
        for r in range(k0, k0 + kn, BF16_ROWS):
            x = s_refs[mp][half_ld, r:r + BF16_ROWS, qcols[qh]]
            er = r if r < SEQ else r + META_PAD
            e_refs[mp][er:er + BF16_ROWS, qcols[qh]] = jnp.exp2(x - mx[it]).astype(BF16)

    def values(it):
        qh, mp = it
        out[it] = jnp.dot(vt_ref[...], e_refs[mp][:, qcols[qh]], preferred_element_type=F32)

    def finish(qh):
        d = 2 * HEAD_DIM
        o1, o2 = out.pop((qh, 0)), out.pop((qh, 1))
        r1 = 1.0 / o1[d:d + 1]
        r2 = lam / o2[d:d + 1]
        o = (o1[0:d] * r1 - o2[0:d] * r2).T
        y = o * lax.rsqrt(jnp.mean(o * o, axis=-1, keepdims=True) + RMS_EPS) * gain_ref[...]
        o_ref[qrows[qh], :] = (y * (1.0 - lambda_init)).astype(BF16)

    def staged(score_it=None, exp_it=None):
        if exp_it is not None:
            col_max(exp_it)
        for ch in chunks:
            if score_it is not None:
                scores(score_it, *ch)
            if exp_it is not None:
                exps(exp_it, *ch)

    items = [(p, mp) for p in range(len(qrows)) for mp in range(2)]
    for n in range(len(items) + 1):
        staged(score_it=items[n] if n < len(items) else None, exp_it=items[n - 1] if n > 0 else None)
        if n > 0:
            values(items[n - 1])
            if items[n - 1][1] == 1:
                finish(items[n - 1][0])


def _diff(lamv, gain, proj3, proj_lead, *, lambda_init, tq):
    b = proj3.shape[0]
    qb, kb, vb = QB_OFF // LANES, KB_OFF // LANES, VB_OFF // LANES
    kern = functools.partial(_diff_kernel, lambda_init=lambda_init, tq=tq)
    return pl.pallas_call(
        kern,
        grid=(b, DIFF_HEADS, SEQ // tq),
        in_specs=[
            pl.BlockSpec((4, HEAD_DIM), lambda i, h, t: (0, 0)),
            pl.BlockSpec((1, LANES), lambda i, h, t: (0, 0)),
            pl.BlockSpec((None, tq, LANES), lambda i, h, t: (i, t, qb + h)),
            pl.BlockSpec((None, SEQ, LANES), lambda i, h, t: (i, 0, kb + h)),
            pl.BlockSpec((None, SEQ, LANES), lambda i, h, t: (i, 0, vb + h)),
            pl.BlockSpec((BLOCK, LANES), lambda i, h, t: (0, kb + h)),
            pl.BlockSpec((BLOCK, LANES), lambda i, h, t: (0, vb + h)),
        ],
        out_specs=pl.BlockSpec((None, tq, LANES), lambda i, h, t: (i, t, h)),
        out_shape=jax.ShapeDtypeStruct((b, SEQ, DIFF_HEADS * LANES), BF16),
        scratch_shapes=[
            pltpu.VMEM((2, KEYS_VALID, LANES), BF16),
            pltpu.VMEM((DIFF_VROWS, KEYS_EXT), BF16),
            pltpu.VMEM((2, KEYS_VALID, 2 * DIFF_PART), F32),
            pltpu.VMEM((2, KEYS_VALID, 2 * DIFF_PART), F32),
            pltpu.VMEM((KEYS_EXT, 2 * DIFF_PART), BF16),
            pltpu.VMEM((KEYS_EXT, 2 * DIFF_PART), BF16),
        ],
        compiler_params=_cparams(3),
        name="diff_attn",
    )(lamv, gain, proj3, proj3, proj3, proj_lead, proj_lead)


MERGE_ROWS = 512

def _merge_kernel(x_ref, os_ref, od_ref, gs_ref, gd_ref, ws_ref, wd_ref, wo_ref, gain_ref, o_ref):
    for r0 in range(0, x_ref.shape[0], MERGE_ROWS):
        rows = slice(r0, r0 + MERGE_ROWS)
        a = jnp.dot(os_ref[rows, :], ws_ref[...], preferred_element_type=F32)
        b = jnp.dot(od_ref[rows, :], wd_ref[...], preferred_element_type=F32)
        merged = gs_ref[rows, :].astype(F32) * a + gd_ref[rows, :].astype(F32) * b
        y = jnp.dot(merged.astype(BF16), wo_ref[...], preferred_element_type=F32)
        o_ref[rows, :] = (x_ref[rows, :]
                          + (y * lax.rsqrt(jnp.mean(y * y, axis=-1, keepdims=True) + RMS_EPS)) * gain_ref[...])


def _merge(x2d, o_swa, o_diff, proj, ws, wd, wo, gain, *, tm):
    n = x2d.shape[0]
    wspec = _resident((D_MODEL, D_MODEL))
    return pl.pallas_call(
        _merge_kernel,
        grid=(n // tm,),
        in_specs=[
            pl.BlockSpec((tm, D_MODEL), lambda i: (i, 0)),
            pl.BlockSpec((tm, D_MODEL), lambda i: (i, 0)),
            pl.BlockSpec((tm, D_MODEL), lambda i: (i, 0)),
            pl.BlockSpec((tm, D_MODEL), lambda i: (i, G_OFF // D_MODEL)),
            pl.BlockSpec((tm, D_MODEL), lambda i: (i, G_OFF // D_MODEL + 1)),
            wspec, wspec, wspec,
            _resident((1, D_MODEL)),
        ],
        out_specs=pl.BlockSpec((tm, D_MODEL), lambda i: (i, 0)),
        out_shape=jax.ShapeDtypeStruct((n, D_MODEL), F32),
        compiler_params=_cparams(1),
        name="merge_out",
    )(x2d, o_swa, o_diff, proj, proj, ws, wd, wo, gain)


FF_CHUNK = 256
FF_ROWS = 512


def _ffn_kernel(h_ref, g1_ref, wi_ref, wo_ref, g2_ref, o_ref, f_ref):
    for r0 in range(0, h_ref.shape[0], FF_ROWS):
        rows = slice(r0, r0 + FF_ROWS)
        h = h_ref[rows, :]
        u = ((h * lax.rsqrt(jnp.mean(h * h, axis=-1, keepdims=True) + RMS_EPS)) * g1_ref[...]).astype(BF16)
        for c in range(D_FF // FF_CHUNK):
            gate = jnp.dot(u, wi_ref[:, c * FF_CHUNK:(c + 1) * FF_CHUNK], preferred_element_type=F32)
            up = jnp.dot(u, wi_ref[:, D_FF + c * FF_CHUNK:D_FF + (c + 1) * FF_CHUNK], preferred_element_type=F32)
            f_ref[rows, c * FF_CHUNK:(c + 1) * FF_CHUNK] = (jax.nn.silu(gate) * up).astype(BF16)
        f = jnp.dot(f_ref[rows, :], wo_ref[...], preferred_element_type=F32)
        o_ref[rows, :] = h + (f * lax.rsqrt(jnp.mean(f * f, axis=-1, keepdims=True) + RMS_EPS)) * g2_ref[...]


def _ffn(h2d, g1, wi, wo, g2, *, tm):
    n = h2d.shape[0]
    return pl.pallas_call(
        _ffn_kernel,
        grid=(n // tm,),
        in_specs=[
            pl.BlockSpec((tm, D_MODEL), lambda i: (i, 0)),
            _resident((1, D_MODEL)),
            _resident((D_MODEL, 2 * D_FF)),
            _resident((D_FF, D_MODEL)),
            _resident((1, D_MODEL)),
        ],
        out_specs=pl.BlockSpec((tm, D_MODEL), lambda i: (i, 0)),
        out_shape=jax.ShapeDtypeStruct((n, D_MODEL), F32),
        scratch_shapes=[pltpu.VMEM((tm, D_FF), BF16)],
        compiler_params=_cparams(1),
        name="ffn",
    )(h2d, g1, wi, wo, g2)


def _permute_cols(a):
    qa, ka, va, rest = a[..., :1024], a[..., 1024:1280], a[..., 1280:1536], a[..., 1536:]
    return jnp.concatenate([qa, rest, ka, va], axis=-1)


def kernel(x, meta_tokens, pre_mix_gain, w_in, b_gate, attn_sink, lambda_q1, lambda_k1, lambda_q2, lambda_k2,
           diff_subln_gain, w_branch_swa, w_branch_diff, w_out, post_mix_gain, pre_ffn_gain, w_ffn_in, w_ffn_out,
           post_ffn_gain):
    bsz = x.shape[0]
    if w_in.shape[0] != 1:
        raise NotImplementedError("DEPTH > 1 is not supported")
    l = 0
    lambda_init = 0.8 - 0.6 * math.exp(-0.3 * l)
    row = lambda a: a.reshape(1, -1).astype(F32)

    x2d = x.reshape(bsz * SEQ, D_MODEL)
    lead = jnp.concatenate([jnp.zeros((META_PAD, D_MODEL), x.dtype), meta_tokens.astype(x.dtype)], axis=0)
    real_tables = _rope_tables(jnp.arange(SEQ, dtype=jnp.int32) + N_META)
    lead_tables = _rope_tables(jnp.arange(BLOCK, dtype=jnp.int32) - META_PAD)

    w = _permute_cols(w_in[l]).astype(BF16)
    proj = _inproj(x2d, row(pre_mix_gain[l]), w, row(b_gate[l]), *real_tables, tm=512)
    proj_lead = _inproj(lead, row(pre_mix_gain[l]), w, row(b_gate[l]), *lead_tables, tm=BLOCK)
    proj3 = proj.reshape(bsz, SEQ, IN_COLS)

    o_swa = _swa(row(attn_sink[l]), proj3, proj_lead)
    lamv = jnp.stack([lambda_q1[l], lambda_k1[l], lambda_q2[l], lambda_k2[l]]).astype(F32)
    o_diff = _diff(lamv, row(diff_subln_gain[l]), proj3, proj_lead, lambda_init=lambda_init, tq=2048)

    h2d = _merge(x2d, o_swa.reshape(bsz * SEQ, D_MODEL), o_diff.reshape(bsz * SEQ, D_MODEL), proj,
                 w_branch_swa[l].astype(BF16), w_branch_diff[l].astype(BF16), w_out[l].astype(BF16),
                 row(post_mix_gain[l]), tm=1024)
    h2d = _ffn(h2d, row(pre_ffn_gain[l]), w_ffn_in[l].astype(BF16), w_ffn_out[l].astype(BF16),
               row(post_ffn_gain[l]), tm=1024)
    return h2d.reshape(bsz, SEQ, D_MODEL)
```

```python
import functools
import math

import jax
import jax.numpy as jnp
from jax import lax
from jax.experimental import pallas as pl
from jax.experimental.pallas import tpu as pltpu

F32 = jnp.float32
BF16 = jnp.bfloat16

D_MODEL = 1024
SEQ = 2048
HEAD_DIM = 64
HALF = HEAD_DIM // 2
BLOCK = 128
N_META = 16
META_PAD = BLOCK - N_META
WINDOW = 128
ROPE_THETA = 10000.0
RMS_EPS = 1e-6
NEG_INF = -1e30
SWA_Q_HEADS = 16
SWA_KV_HEADS = 4
DIFF_HEADS = 8
D_FF = 2816
LANES = 128
F32_ROWS = 8
BF16_ROWS = 16

QA_OFF, QB_OFF, KB_OFF, VB_OFF, G_OFF, KA_OFF, VA_OFF = 0, 1024, 2048, 3072, 4096, 6144, 6400
IN_COLS = 6656
PROJ_GROUP = 512
BAND = 3 * BLOCK
KEYS_EXT = SEQ + BLOCK
KEYS_VALID = SEQ + N_META

LOG2E = math.log2(math.e)
Q_SCALE = HEAD_DIM ** -0.5 * LOG2E

VMEM_LIMIT = 56 * 1024 * 1024

TOKENS_PER_STEP = 1024
ROWS_PER_PASS = 512
DIFF_QUERIES_PER_STEP = SEQ


def _cparams(n_axes):
    return pltpu.CompilerParams(dimension_semantics=("arbitrary",) * n_axes, vmem_limit_bytes=VMEM_LIMIT)


def _nt_dot(a, b):
    return lax.dot_general(a, b, (((1,), (1,)), ((), ())), preferred_element_type=F32)


def _resident(shape):
    return pl.BlockSpec(shape, lambda *_: (0,) * len(shape), pipeline_mode=pl.Buffered(1))


def _col_kind(col):
    if col < KB_OFF:
        return "rope", Q_SCALE
    if col < VB_OFF:
        return "rope", 1.0
    if col < G_OFF:
        return "plain", 1.0
    if col < KA_OFF:
        return "gate", 1.0
    if col < VA_OFF:
        return "rope", 1.0
    return "plain", 1.0


def _inproj_kernel(x_ref, g_ref, w_ref, b_ref, cos_ref, sa_ref, sb_ref, o_ref):
    tm = x_ref.shape[0]
    for r0 in range(0, tm, min(tm, ROWS_PER_PASS)):
        rows = slice(r0, r0 + min(tm, ROWS_PER_PASS))
        xx = x_ref[rows, :]
        ms = jnp.mean(xx * xx, axis=-1, keepdims=True)
        u = ((xx * lax.rsqrt(ms + RMS_EPS)) * g_ref[...]).astype(BF16)
        cos, sa, sb = cos_ref[rows, :], sa_ref[rows, :], sb_ref[rows, :]
        for grp in range(IN_COLS // PROJ_GROUP):
            c0 = grp * PROJ_GROUP
            acc = jnp.dot(u, w_ref[:, c0:c0 + PROJ_GROUP], preferred_element_type=F32)
            for ci in range(PROJ_GROUP // LANES):
                col = c0 + ci * LANES
                t = acc[:, ci * LANES:(ci + 1) * LANES]
                kind, scale = _col_kind(col)
                if kind == "rope":
                    t = t * cos + pltpu.roll(t, LANES - HALF, 1) * sa + pltpu.roll(t, HALF, 1) * sb
                    if scale != 1.0:
                        t = t * scale
                elif kind == "gate":
                    t = jax.nn.sigmoid(t + b_ref[:, col - G_OFF:col - G_OFF + LANES])
                o_ref[rows, col:col + LANES] = t.astype(BF16)


def _inproj(x2d, gain, w, b_gate, cos, sa, sb, *, tm):
    n = x2d.shape[0]
    nt = cos.shape[0] // tm
    table = pl.BlockSpec((tm, LANES), lambda i: (i % nt, 0))
    return pl.pallas_call(
        _inproj_kernel,
        grid=(n // tm,),
        in_specs=[
            pl.BlockSpec((tm, D_MODEL), lambda i: (i, 0)),
            _resident((1, D_MODEL)),
            _resident((D_MODEL, IN_COLS)),
            _resident((1, 2 * D_MODEL)),
            table, table, table,
        ],
        out_specs=pl.BlockSpec((tm, IN_COLS), lambda i: (i, 0)),
        out_shape=jax.ShapeDtypeStruct((n, IN_COLS), BF16),
        compiler_params=_cparams(1),
        name="inproj",
    )(x2d, gain, w, b_gate, cos, sa, sb)


def _rope_tables(pos):
    inv_freq = ROPE_THETA ** (-jnp.arange(0, HEAD_DIM, 2, dtype=F32) / HEAD_DIM)
    ang = pos.astype(F32)[:, None] * inv_freq[None, :]
    lane = jnp.arange(LANES)
    cos = jnp.cos(ang)[:, lane % HALF]
    sin = jnp.sin(ang)[:, lane % HALF]
    first = (lane % HEAD_DIM) < HALF
    return cos, jnp.where(first, -sin, 0.0), jnp.where(first, 0.0, sin)


SWA_VROWS = HEAD_DIM + BF16_ROWS
SWA_KROWS = BAND + N_META
SWA_UNROLL = 4
SWA_VALUE_LAG = 5


def _swa_kernel(sink_ref, q_ref, k_ref, v_ref, km_ref, vm_ref, o_ref, kx_ref, kmx_ref, vt_ref):
    lo = lax.broadcasted_iota(jnp.int32, (1, LANES), 1) < HEAD_DIM

    def expand_k(src, dst, sl_src, sl_dst):
        for lb in range(2):
            blk = src[sl_src, lb * LANES:(lb + 1) * LANES].astype(F32)
            rolled = pltpu.roll(blk, HEAD_DIM, 1)
            g0, g1 = 2 * lb, 2 * lb + 1
            dst[2 * g0, sl_dst, :] = jnp.where(lo, blk, 0.0).astype(BF16)
            dst[2 * g0 + 1, sl_dst, :] = jnp.where(lo, 0.0, rolled).astype(BF16)
            dst[2 * g1, sl_dst, :] = jnp.where(lo, rolled, 0.0).astype(BF16)
            dst[2 * g1 + 1, sl_dst, :] = jnp.where(lo, 0.0, blk).astype(BF16)

    def put_vt(src, sl_src, sl_dst):
        vt = src[sl_src, :].astype(F32).T
        for g in range(SWA_KV_HEADS):
            vt_ref[g, 0:HEAD_DIM, sl_dst] = vt[g * HEAD_DIM:(g + 1) * HEAD_DIM].astype(BF16)

    def prep_body(c, carry):
        sl = pl.ds(pl.multiple_of(c * 256, 256), 256)
        expand_k(k_ref, kx_ref, sl, sl)
        put_vt(v_ref, sl, sl)
        return carry
    lax.fori_loop(0, SEQ // 256, prep_body, 0)
    expand_k(km_ref, kmx_ref, slice(META_PAD, BLOCK), slice(None))
    put_vt(vm_ref, slice(None), slice(SEQ, KEYS_EXT))
    vt_ref[:, HEAD_DIM:SWA_VROWS, :] = jnp.ones((SWA_KV_HEADS, BF16_ROWS, KEYS_EXT), BF16)

    krow = lax.broadcasted_iota(jnp.int32, (SWA_KROWS, 2 * LANES), 0)
    kq_delta = krow - lax.broadcasted_iota(jnp.int32, (SWA_KROWS, 2 * LANES), 1) % LANES
    first_blk = lax.broadcasted_iota(jnp.int32, (1, 2 * LANES), 1) < LANES

    filler_rows = jnp.zeros((META_PAD, 2 * LANES), BF16)

    def body(j, carry):
        q0, start, bias = [], [], []
        for i in range(SWA_UNROLL):
            r = j * SWA_UNROLL + i
            q0.append(pl.multiple_of(r * BLOCK, BLOCK))
            start.append(pl.multiple_of(jnp.clip(r * BLOCK - BLOCK, 0, SEQ - BAND), BLOCK))
            visible = (krow >= BAND) | (jnp.abs(kq_delta + (start[i] - q0[i])) <= WINDOW)
            bias.append(jnp.where(visible, 0.0, NEG_INF).astype(F32))

        chains = [(i, g, v) for i in range(SWA_UNROLL) for g in range(SWA_KV_HEADS) for v in range(2)]

        def scores(i, g, v):
            lb0, lb1 = 2 * g, 2 * g + 1
            qq = jnp.concatenate([q_ref[pl.ds(q0[i], BLOCK), lb0 * LANES:(lb0 + 1) * LANES],
                                  q_ref[pl.ds(q0[i], BLOCK), lb1 * LANES:(lb1 + 1) * LANES]], axis=0)
            kcat = jnp.concatenate([kx_ref[2 * g + v, pl.ds(start[i], BAND), :], kmx_ref[2 * g + v]], axis=0)
            sink = jnp.where(first_blk, sink_ref[0, 2 * lb0 + v], sink_ref[0, 2 * lb1 + v]) * LOG2E
            return _nt_dot(kcat, qq), sink, i

        def masked_max(s, sink, i):
            s = s + bias[i]
            return s, jnp.maximum(jnp.max(s, axis=0, keepdims=True), sink), sink

        def probs(s, m, sink):
            p = jnp.exp2(s - m).astype(BF16)
            return jnp.concatenate([p[0:BAND], filler_rows, p[BAND:SWA_KROWS]], axis=0), jnp.exp2(sink - m)

        def values(i, g, pt, psink):
            vcat = jnp.concatenate([vt_ref[g, :, pl.ds(start[i], BAND)], vt_ref[g, :, SEQ:KEYS_EXT]], axis=1)
            o = jnp.dot(vcat, pt, preferred_element_type=F32)
            return o[0:HEAD_DIM] * (1.0 / (o[HEAD_DIM:HEAD_DIM + 1] + psink))

        st_a, st_b, st_c, st_d = {}, {}, {}, {}
        for step in range(len(chains) + SWA_VALUE_LAG):
            if step < len(chains):
                st_a[step] = scores(*chains[step])
            if 0 <= step - 1 < len(chains):
                st_b[step - 1] = masked_max(*st_a.pop(step - 1))
            if 0 <= step - 2 < len(chains):
                st_c[step - 2] = probs(*st_b.pop(step - 2))
            if 0 <= step - SWA_VALUE_LAG < len(chains):
                i, g, v = chains[step - SWA_VALUE_LAG]
                st_d[v] = values(i, g, *st_c.pop(step - SWA_VALUE_LAG))
                if v == 1:
                    ot = jnp.concatenate([st_d.pop(0), st_d.pop(1)], axis=0)
                    for half, lb in enumerate((2 * g, 2 * g + 1)):
                        o_ref[pl.ds(q0[i], BLOCK), lb * LANES:(lb + 1) * LANES] = (
                            ot[:, half * LANES:(half + 1) * LANES].T.astype(BF16))
        return carry
    lax.fori_loop(0, SEQ // (BLOCK * SWA_UNROLL), body, 0)


def _swa(sink, proj3, proj_lead):
    b = proj3.shape[0]
    return pl.pallas_call(
        _swa_kernel,
        grid=(b,),
        in_specs=[
            pl.BlockSpec(memory_space=pltpu.SMEM),
            pl.BlockSpec((None, SEQ, 1024), lambda i: (i, 0, QA_OFF // 1024)),
            pl.BlockSpec((None, SEQ, 256), lambda i: (i, 0, KA_OFF // 256)),
            pl.BlockSpec((None, SEQ, 256), lambda i: (i, 0, VA_OFF // 256)),
            pl.BlockSpec((BLOCK, 256), lambda i: (0, KA_OFF // 256)),
            pl.BlockSpec((BLOCK, 256), lambda i: (0, VA_OFF // 256)),
        ],
        out_specs=pl.BlockSpec((None, SEQ, 1024), lambda i: (i, 0, 0)),
        out_shape=jax.ShapeDtypeStruct((b, SEQ, 1024), BF16),
        scratch_shapes=[
            pltpu.VMEM((2 * SWA_KV_HEADS, SEQ, LANES), BF16),
            pltpu.VMEM((2 * SWA_KV_HEADS, N_META, LANES), BF16),
            pltpu.VMEM((SWA_KV_HEADS, SWA_VROWS, KEYS_EXT), BF16),
        ],
        compiler_params=_cparams(1),
        name="swa_attn",
    )(sink, proj3, proj3, proj3, proj_lead, proj_lead)


DIFF_VROWS = 2 * HEAD_DIM + BF16_ROWS
DIFF_KCHUNK = 512
DIFF_PART = 512


def _diff_kernel(lamv_ref, gain_ref, q_ref, k_ref, v_ref, km_ref, vm_ref, o_ref, kx_ref, vt_ref,
                 s0_ref, s1_ref, e0_ref, e1_ref, *, lambda_init, tq):
    s_refs, e_refs = (s0_ref, s1_ref), (e0_ref, e1_ref)

    @pl.when(pl.program_id(2) == 0)
    def _():
        lo = lax.broadcasted_iota(jnp.int32, (1, LANES), 1) < HEAD_DIM
        zero = jnp.zeros((), BF16)

        def put_k(kk, sl):
            kx_ref[0, sl, :] = jnp.where(lo, kk, zero)
            kx_ref[1, sl, :] = jnp.where(lo, zero, kk)
        for c in range(SEQ // 512):
            put_k(k_ref[c * 512:(c + 1) * 512, :], slice(c * 512, (c + 1) * 512))
            for cc in range(2):
                sl = slice(c * 512 + cc * 256, c * 512 + (cc + 1) * 256)
                vt_ref[0:2 * HEAD_DIM, sl] = v_ref[sl, :].astype(F32).T.astype(BF16)
        put_k(km_ref[META_PAD:BLOCK, :], slice(SEQ, KEYS_VALID))
        vt_ref[0:2 * HEAD_DIM, SEQ:KEYS_EXT] = vm_ref[...].astype(F32).T.astype(BF16)
        vt_ref[2 * HEAD_DIM:DIFF_VROWS, :] = jnp.ones((BF16_ROWS, KEYS_EXT), BF16)
        for e_ref in e_refs:
            e_ref[SEQ:SEQ + META_PAD, :] = jnp.zeros((META_PAD, 2 * DIFF_PART), BF16)

    lv = lamv_ref[...]
    lam = (jnp.exp(jnp.sum(lv[0:1] * lv[1:2], axis=-1, keepdims=True))
           - jnp.exp(jnp.sum(lv[2:3] * lv[3:4], axis=-1, keepdims=True)) + lambda_init)
    chunks = [(k0, DIFF_KCHUNK if k0 + DIFF_KCHUNK < SEQ else KEYS_VALID - k0) for k0 in range(0, SEQ, DIFF_KCHUNK)]
    qrows = [slice(p * DIFF_PART, (p + 1) * DIFF_PART) for p in range(tq // DIFF_PART)]
    qcols = [slice((p % 2) * DIFF_PART, (p % 2 + 1) * DIFF_PART) for p in range(tq // DIFF_PART)]

    t = pl.program_id(1) * pl.num_programs(2) + pl.program_id(2)
    half_st = t & 1
    half_ld = t - ((t >> 1) << 1)

    acc, mx, out = {}, {}, {}

    def scores(it, k0, kn):
        qh, mp = it
        s = _nt_dot(kx_ref[mp, k0:k0 + kn, :], q_ref[qrows[qh], :])
        s_refs[mp][half_st, k0:k0 + kn, qcols[qh]] = s
        a = acc.setdefault(it, [None, None])
        for i, r in enumerate(range(0, kn, F32_ROWS)):
            x = s[r:r + F32_ROWS, :]
            a[i % 2] = x if a[i % 2] is None else jnp.maximum(a[i % 2], x)

    def col_max(it):
        m8 = jnp.maximum(*acc.pop(it))
        mx[it] = jnp.broadcast_to(jnp.max(m8, axis=0, keepdims=True), (BF16_ROWS, m8.shape[1]))

    def exps(it, k0, kn):
        qh, mp = it
        for r in range(k0, k0 + kn, BF16_ROWS):
            x = s_refs[mp][half_ld, r:r + BF16_ROWS, qcols[qh]]
            er = r if r < SEQ else r + META_PAD
            e_refs[mp][er:er + BF16_ROWS, qcols[qh]] = jnp.exp2(x - mx[it]).astype(BF16)

    def values(it):
        qh, mp = it
        out[it] = jnp.dot(vt_ref[...], e_refs[mp][:, qcols[qh]], preferred_element_type=F32)

    def finish(qh):
        d = 2 * HEAD_DIM
        o1, o2 = out.pop((qh, 0)), out.pop((qh, 1))
        r1 = 1.0 / o1[d:d + 1]
        r2 = lam / o2[d:d + 1]
        o = (o1[0:d] * r1 - o2[0:d] * r2).T
        y = o * lax.rsqrt(jnp.mean(o * o, axis=-1, keepdims=True) + RMS_EPS) * gain_ref[...]
        o_ref[qrows[qh], :] = (y * (1.0 - lambda_init)).astype(BF16)

    def staged(score_it=None, exp_it=None):
        if exp_it is not None:
            col_max(exp_it)
        for ch in chunks:
            if score_it is not None:
                scores(score_it, *ch)
            if exp_it is not None:
                exps(exp_it, *ch)

    items = [(p, mp) for p in range(len(qrows)) for mp in range(2)]
    for n in range(len(items) + 1):
        staged(score_it=items[n] if n < len(items) else None, exp_it=items[n - 1] if n > 0 else None)
        if n > 0:
            values(items[n - 1])
            if items[n - 1][1] == 1:
                finish(items[n - 1][0])


def _diff(lamv, gain, proj3, proj_lead, *, lambda_init, tq):
    b = proj3.shape[0]
    qb, kb, vb = QB_OFF // LANES, KB_OFF // LANES, VB_OFF // LANES
    kern = functools.partial(_diff_kernel, lambda_init=lambda_init, tq=tq)
    return pl.pallas_call(
        kern,
        grid=(b, DIFF_HEADS, SEQ // tq),
        in_specs=[
            pl.BlockSpec((4, HEAD_DIM), lambda i, h, t: (0, 0)),
            pl.BlockSpec((1, LANES), lambda i, h, t: (0, 0)),
            pl.BlockSpec((None, tq, LANES), lambda i, h, t: (i, t, qb + h)),
            pl.BlockSpec((None, SEQ, LANES), lambda i, h, t: (i, 0, kb + h)),
            pl.BlockSpec((None, SEQ, LANES), lambda i, h, t: (i, 0, vb + h)),
            pl.BlockSpec((BLOCK, LANES), lambda i, h, t: (0, kb + h)),
            pl.BlockSpec((BLOCK, LANES), lambda i, h, t: (0, vb + h)),
        ],
        out_specs=pl.BlockSpec((None, tq, LANES), lambda i, h, t: (i, t, h)),
        out_shape=jax.ShapeDtypeStruct((b, SEQ, DIFF_HEADS * LANES), BF16),
        scratch_shapes=[
            pltpu.VMEM((2, KEYS_VALID, LANES), BF16),
            pltpu.VMEM((DIFF_VROWS, KEYS_EXT), BF16),
            pltpu.VMEM((2, KEYS_VALID, 2 * DIFF_PART), F32),
            pltpu.VMEM((2, KEYS_VALID, 2 * DIFF_PART), F32),
            pltpu.VMEM((KEYS_EXT, 2 * DIFF_PART), BF16),
            pltpu.VMEM((KEYS_EXT, 2 * DIFF_PART), BF16),
        ],
        compiler_params=_cparams(3),
        name="diff_attn",
    )(lamv, gain, proj3, proj3, proj3, proj_lead, proj_lead)


def _merge_kernel(x_ref, os_ref, od_ref, gs_ref, gd_ref, ws_ref, wd_ref, wo_ref, gain_ref, o_ref):
    for r0 in range(0, x_ref.shape[0], ROWS_PER_PASS):
        rows = slice(r0, r0 + ROWS_PER_PASS)
        a = jnp.dot(os_ref[rows, :], ws_ref[...], preferred_element_type=F32)
        b = jnp.dot(od_ref[rows, :], wd_ref[...], preferred_element_type=F32)
        merged = gs_ref[rows, :].astype(F32) * a + gd_ref[rows, :].astype(F32) * b
        y = jnp.dot(merged.astype(BF16), wo_ref[...], preferred_element_type=F32)
        o_ref[rows, :] = (x_ref[rows, :]
                          + (y * lax.rsqrt(jnp.mean(y * y, axis=-1, keepdims=True) + RMS_EPS)) * gain_ref[...])


def _merge(x2d, o_swa, o_diff, proj, ws, wd, wo, gain, *, tm):
    n = x2d.shape[0]
    wspec = _resident((D_MODEL, D_MODEL))
    return pl.pallas_call(
        _merge_kernel,
        grid=(n // tm,),
        in_specs=[
            pl.BlockSpec((tm, D_MODEL), lambda i: (i, 0)),
            pl.BlockSpec((tm, D_MODEL), lambda i: (i, 0)),
            pl.BlockSpec((tm, D_MODEL), lambda i: (i, 0)),
            pl.BlockSpec((tm, D_MODEL), lambda i: (i, G_OFF // D_MODEL)),
            pl.BlockSpec((tm, D_MODEL), lambda i: (i, G_OFF // D_MODEL + 1)),
            wspec, wspec, wspec,
            _resident((1, D_MODEL)),
        ],
        out_specs=pl.BlockSpec((tm, D_MODEL), lambda i: (i, 0)),
        out_shape=jax.ShapeDtypeStruct((n, D_MODEL), F32),
        compiler_params=_cparams(1),
        name="merge_out",
    )(x2d, o_swa, o_diff, proj, proj, ws, wd, wo, gain)


FF_CHUNK = 256


def _ffn_kernel(h_ref, g1_ref, wi_ref, wo_ref, g2_ref, o_ref, f_ref):
    for r0 in range(0, h_ref.shape[0], ROWS_PER_PASS):
        rows = slice(r0, r0 + ROWS_PER_PASS)
        h = h_ref[rows, :]
        u = ((h * lax.rsqrt(jnp.mean(h * h, axis=-1, keepdims=True) + RMS_EPS)) * g1_ref[...]).astype(BF16)
        for c in range(D_FF // FF_CHUNK):
            gate = jnp.dot(u, wi_ref[:, c * FF_CHUNK:(c + 1) * FF_CHUNK], preferred_element_type=F32)
            up = jnp.dot(u, wi_ref[:, D_FF + c * FF_CHUNK:D_FF + (c + 1) * FF_CHUNK], preferred_element_type=F32)
            f_ref[rows, c * FF_CHUNK:(c + 1) * FF_CHUNK] = (jax.nn.silu(gate) * up).astype(BF16)
        f = jnp.dot(f_ref[rows, :], wo_ref[...], preferred_element_type=F32)
        o_ref[rows, :] = h + (f * lax.rsqrt(jnp.mean(f * f, axis=-1, keepdims=True) + RMS_EPS)) * g2_ref[...]


def _ffn(h2d, g1, wi, wo, g2, *, tm):
    n = h2d.shape[0]
    return pl.pallas_call(
        _ffn_kernel,
        grid=(n // tm,),
        in_specs=[
            pl.BlockSpec((tm, D_MODEL), lambda i: (i, 0)),
            _resident((1, D_MODEL)),
            _resident((D_MODEL, 2 * D_FF)),
            _resident((D_FF, D_MODEL)),
            _resident((1, D_MODEL)),
        ],
        out_specs=pl.BlockSpec((tm, D_MODEL), lambda i: (i, 0)),
        out_shape=jax.ShapeDtypeStruct((n, D_MODEL), F32),
        scratch_shapes=[pltpu.VMEM((tm, D_FF), BF16)],
        compiler_params=_cparams(1),
        name="ffn",
    )(h2d, g1, wi, wo, g2)


def _permute_cols(a):
    qa, ka, va, rest = a[..., :1024], a[..., 1024:1280], a[..., 1280:1536], a[..., 1536:]
    return jnp.concatenate([qa, rest, ka, va], axis=-1)


def kernel(x, meta_tokens, pre_mix_gain, w_in, b_gate, attn_sink, lambda_q1, lambda_k1, lambda_q2, lambda_k2,
           diff_subln_gain, w_branch_swa, w_branch_diff, w_out, post_mix_gain, pre_ffn_gain, w_ffn_in, w_ffn_out,
           post_ffn_gain):
    bsz = x.shape[0]
    if w_in.shape[0] != 1:
        raise NotImplementedError("DEPTH > 1 is not supported")
    l = 0
    lambda_init = 0.8 - 0.6 * math.exp(-0.3 * l)
    row = lambda a: a.reshape(1, -1).astype(F32)

    x2d = x.reshape(bsz * SEQ, D_MODEL)
    lead = jnp.concatenate([jnp.zeros((META_PAD, D_MODEL), x.dtype), meta_tokens.astype(x.dtype)], axis=0)
    real_tables = _rope_tables(jnp.arange(SEQ, dtype=jnp.int32) + N_META)
    lead_tables = _rope_tables(jnp.arange(BLOCK, dtype=jnp.int32) - META_PAD)

    w = _permute_cols(w_in[l]).astype(BF16)
    proj = _inproj(x2d, row(pre_mix_gain[l]), w, row(b_gate[l]), *real_tables, tm=TOKENS_PER_STEP)
    proj_lead = _inproj(lead, row(pre_mix_gain[l]), w, row(b_gate[l]), *lead_tables, tm=BLOCK)
    proj3 = proj.reshape(bsz, SEQ, IN_COLS)

    o_swa = _swa(row(attn_sink[l]), proj3, proj_lead)
    lamv = jnp.stack([lambda_q1[l], lambda_k1[l], lambda_q2[l], lambda_k2[l]]).astype(F32)
    o_diff = _diff(lamv, row(diff_subln_gain[l]), proj3, proj_lead, lambda_init=lambda_init,
                   tq=DIFF_QUERIES_PER_STEP)

    h2d = _merge(x2d, o_swa.reshape(bsz * SEQ, D_MODEL), o_diff.reshape(bsz * SEQ, D_MODEL), proj,
                 w_branch_swa[l].astype(BF16), w_branch_diff[l].astype(BF16), w_out[l].astype(BF16),
                 row(post_mix_gain[l]), tm=TOKENS_PER_STEP)
    h2d = _ffn(h2d, row(pre_ffn_gain[l]), w_ffn_in[l].astype(BF16), w_ffn_out[l].astype(BF16),
               row(post_ffn_gain[l]), tm=TOKENS_PER_STEP)
    return h2d.reshape(bsz, SEQ, D_MODEL)
```

```python
import functools
import math

import jax
import jax.numpy as jnp
from jax import lax
from jax.experimental import pallas as pl
from jax.experimental.pallas import tpu as pltpu

F32 = jnp.float32
BF16 = jnp.bfloat16

D_MODEL = 1024
SEQ = 2048
HEAD_DIM = 64
HALF = HEAD_DIM // 2
BLOCK = 128
N_META = 16
META_PAD = BLOCK - N_META
WINDOW = 128
ROPE_THETA = 10000.0
RMS_EPS = 1e-6
NEG_INF = -1e30
SWA_Q_HEADS = 16
SWA_KV_HEADS = 4
DIFF_HEADS = 8
D_FF = 2816
LANES = 128
F32_ROWS = 8
BF16_ROWS = 16

QA_OFF, QB_OFF, KB_OFF, VB_OFF, G_OFF, KA_OFF, VA_OFF = 0, 1024, 2048, 3072, 4096, 6144, 6400
IN_COLS = 6656
PROJ_GROUP = 512
BAND = 3 * BLOCK
KEYS_EXT = SEQ + BLOCK
KEYS_VALID = SEQ + N_META

LOG2E = math.log2(math.e)
Q_SCALE = HEAD_DIM ** -0.5 * LOG2E

VMEM_LIMIT = 56 * 1024 * 1024

TOKENS_PER_STEP = 1024
ROWS_PER_PASS = 512
DIFF_QUERIES_PER_STEP = SEQ


def _cparams(n_axes):
    return pltpu.CompilerParams(dimension_semantics=("arbitrary",) * n_axes, vmem_limit_bytes=VMEM_LIMIT)


def _nt_dot(a, b):
    return lax.dot_general(a, b, (((1,), (1,)), ((), ())), preferred_element_type=F32)


def _resident(shape):
    return pl.BlockSpec(shape, lambda *_: (0,) * len(shape), pipeline_mode=pl.Buffered(1))


def _col_kind(col):
    if col < KB_OFF:
        return "rope", Q_SCALE
    if col < VB_OFF:
        return "rope", 1.0
    if col < G_OFF:
        return "plain", 1.0
    if col < KA_OFF:
        return "gate", 1.0
    if col < VA_OFF:
        return "rope", 1.0
    return "plain", 1.0


def _inproj_kernel(x_ref, g_ref, w_ref, b_ref, cos_ref, sa_ref, sb_ref, o_ref):
    tm = x_ref.shape[0]
    for r0 in range(0, tm, min(tm, ROWS_PER_PASS)):
        rows = slice(r0, r0 + min(tm, ROWS_PER_PASS))
        xx = x_ref[rows, :]
        ms = jnp.mean(xx * xx, axis=-1, keepdims=True)
        u = ((xx * lax.rsqrt(ms + RMS_EPS)) * g_ref[...]).astype(BF16)
        cos, sa, sb = cos_ref[rows, :], sa_ref[rows, :], sb_ref[rows, :]
        for grp in range(IN_COLS // PROJ_GROUP):
            c0 = grp * PROJ_GROUP
            acc = jnp.dot(u, w_ref[:, c0:c0 + PROJ_GROUP], preferred_element_type=F32)
            for ci in range(PROJ_GROUP // LANES):
                col = c0 + ci * LANES
                t = acc[:, ci * LANES:(ci + 1) * LANES]
                kind, scale = _col_kind(col)
                if kind == "rope":
                    t = t * cos + pltpu.roll(t, LANES - HALF, 1) * sa + pltpu.roll(t, HALF, 1) * sb
                    if scale != 1.0:
                        t = t * scale
                elif kind == "gate":
                    t = jax.nn.sigmoid(t + b_ref[:, col - G_OFF:col - G_OFF + LANES])
                o_ref[rows, col:col + LANES] = t.astype(BF16)


def _inproj(x2d, gain, w, b_gate, cos, sa, sb, *, tm):
    n = x2d.shape[0]
    nt = cos.shape[0] // tm
    table = pl.BlockSpec((tm, LANES), lambda i: (i % nt, 0))
    return pl.pallas_call(
        _inproj_kernel,
        grid=(n // tm,),
        in_specs=[
            pl.BlockSpec((tm, D_MODEL), lambda i: (i, 0)),
            _resident((1, D_MODEL)),
            _resident((D_MODEL, IN_COLS)),
            _resident((1, 2 * D_MODEL)),
            table, table, table,
        ],
        out_specs=pl.BlockSpec((tm, IN_COLS), lambda i: (i, 0)),
        out_shape=jax.ShapeDtypeStruct((n, IN_COLS), BF16),
        compiler_params=_cparams(1),
        name="inproj",
    )(x2d, gain, w, b_gate, cos, sa, sb)


def _rope_tables(pos):
    inv_freq = ROPE_THETA ** (-jnp.arange(0, HEAD_DIM, 2, dtype=F32) / HEAD_DIM)
    ang = pos.astype(F32)[:, None] * inv_freq[None, :]
    lane = jnp.arange(LANES)
    cos = jnp.cos(ang)[:, lane % HALF]
    sin = jnp.sin(ang)[:, lane % HALF]
    first = (lane % HEAD_DIM) < HALF
    return cos, jnp.where(first, -sin, 0.0), jnp.where(first, 0.0, sin)


SWA_VROWS = HEAD_DIM + BF16_ROWS
SWA_KROWS = BAND + N_META
SWA_UNROLL = 4
SWA_VALUE_LAG = 5


def _swa_kernel(sink_ref, q_ref, k_ref, v_ref, km_ref, vm_ref, o_ref, kx_ref, kmx_ref, vt_ref):
    lo = lax.broadcasted_iota(jnp.int32, (1, LANES), 1) < HEAD_DIM

    def expand_k(src, dst, sl_src, sl_dst):
        for lb in range(2):
            blk = src[sl_src, lb * LANES:(lb + 1) * LANES].astype(F32)
            rolled = pltpu.roll(blk, HEAD_DIM, 1)
            g0, g1 = 2 * lb, 2 * lb + 1
            dst[2 * g0, sl_dst, :] = jnp.where(lo, blk, 0.0).astype(BF16)
            dst[2 * g0 + 1, sl_dst, :] = jnp.where(lo, 0.0, rolled).astype(BF16)
            dst[2 * g1, sl_dst, :] = jnp.where(lo, rolled, 0.0).astype(BF16)
            dst[2 * g1 + 1, sl_dst, :] = jnp.where(lo, 0.0, blk).astype(BF16)

    def put_vt(src, sl_src, sl_dst):
        vt = src[sl_src, :].astype(F32).T
        for g in range(SWA_KV_HEADS):
            vt_ref[g, 0:HEAD_DIM, sl_dst] = vt[g * HEAD_DIM:(g + 1) * HEAD_DIM].astype(BF16)

    def prep_body(c, carry):
        sl = pl.ds(pl.multiple_of(c * 256, 256), 256)
        expand_k(k_ref, kx_ref, sl, sl)
        put_vt(v_ref, sl, sl)
        return carry
    lax.fori_loop(0, SEQ // 256, prep_body, 0)
    expand_k(km_ref, kmx_ref, slice(META_PAD, BLOCK), slice(None))
    put_vt(vm_ref, slice(None), slice(SEQ, KEYS_EXT))
    vt_ref[:, HEAD_DIM:SWA_VROWS, :] = jnp.ones((SWA_KV_HEADS, BF16_ROWS, KEYS_EXT), BF16)

    krow = lax.broadcasted_iota(jnp.int32, (SWA_KROWS, 2 * LANES), 0)
    kq_delta = krow - lax.broadcasted_iota(jnp.int32, (SWA_KROWS, 2 * LANES), 1) % LANES
    first_blk = lax.broadcasted_iota(jnp.int32, (1, 2 * LANES), 1) < LANES

    filler_rows = jnp.zeros((META_PAD, 2 * LANES), BF16)

    def body(j, carry):
        q0, start, bias = [], [], []
        for i in range(SWA_UNROLL):
            r = j * SWA_UNROLL + i
            q0.append(pl.multiple_of(r * BLOCK, BLOCK))
            start.append(pl.multiple_of(jnp.clip(r * BLOCK - BLOCK, 0, SEQ - BAND), BLOCK))
            visible = (krow >= BAND) | (jnp.abs(kq_delta + (start[i] - q0[i])) <= WINDOW)
            bias.append(jnp.where(visible, 0.0, NEG_INF).astype(F32))

        chains = [(i, g, v) for i in range(SWA_UNROLL) for g in range(SWA_KV_HEADS) for v in range(2)]

        def scores(i, g, v):
            lb0, lb1 = 2 * g, 2 * g + 1
            qq = jnp.concatenate([q_ref[pl.ds(q0[i], BLOCK), lb0 * LANES:(lb0 + 1) * LANES],
                                  q_ref[pl.ds(q0[i], BLOCK), lb1 * LANES:(lb1 + 1) * LANES]], axis=0)
            kcat = jnp.concatenate([kx_ref[2 * g + v, pl.ds(start[i], BAND), :], kmx_ref[2 * g + v]], axis=0)
            sink = jnp.where(first_blk, sink_ref[0, 2 * lb0 + v], sink_ref[0, 2 * lb1 + v]) * LOG2E
            return _nt_dot(kcat, qq), sink, i

        def masked_max(s, sink, i):
            s = s + bias[i]
            return s, jnp.maximum(jnp.max(s, axis=0, keepdims=True), sink), sink

        def probs(s, m, sink):
            p = jnp.exp2(s - m).astype(BF16)
            return jnp.concatenate([p[0:BAND], filler_rows, p[BAND:SWA_KROWS]], axis=0), jnp.exp2(sink - m)

        def values(i, g, pt, psink):
            vcat = jnp.concatenate([vt_ref[g, :, pl.ds(start[i], BAND)], vt_ref[g, :, SEQ:KEYS_EXT]], axis=1)
            o = jnp.dot(vcat, pt, preferred_element_type=F32)
            return o[0:HEAD_DIM] * (1.0 / (o[HEAD_DIM:HEAD_DIM + 1] + psink))

        st_a, st_b, st_c, st_d = {}, {}, {}, {}
        for step in range(len(chains) + SWA_VALUE_LAG):
            if step < len(chains):
                st_a[step] = scores(*chains[step])
            if 0 <= step - 1 < len(chains):
                st_b[step - 1] = masked_max(*st_a.pop(step - 1))
            if 0 <= step - 2 < len(chains):
                st_c[step - 2] = probs(*st_b.pop(step - 2))
            if 0 <= step - SWA_VALUE_LAG < len(chains):
                i, g, v = chains[step - SWA_VALUE_LAG]
                st_d[v] = values(i, g, *st_c.pop(step - SWA_VALUE_LAG))
                if v == 1:
                    ot = jnp.concatenate([st_d.pop(0), st_d.pop(1)], axis=0)
                    for half, lb in enumerate((2 * g, 2 * g + 1)):
                        o_ref[pl.ds(q0[i], BLOCK), lb * LANES:(lb + 1) * LANES] = (
                            ot[:, half * LANES:(half + 1) * LANES].T.astype(BF16))
        return carry
    lax.fori_loop(0, SEQ // (BLOCK * SWA_UNROLL), body, 0)


def _swa(sink, proj3, proj_lead):
    b = proj3.shape[0]
    return pl.pallas_call(
        _swa_kernel,
        grid=(b,),
        in_specs=[
            pl.BlockSpec(memory_space=pltpu.SMEM),
            pl.BlockSpec((None, SEQ, 1024), lambda i: (i, 0, QA_OFF // 1024)),
            pl.BlockSpec((None, SEQ, 256), lambda i: (i, 0, KA_OFF // 256)),
            pl.BlockSpec((None, SEQ, 256), lambda i: (i, 0, VA_OFF // 256)),
            pl.BlockSpec((BLOCK, 256), lambda i: (0, KA_OFF // 256)),
            pl.BlockSpec((BLOCK, 256), lambda i: (0, VA_OFF // 256)),
        ],
        out_specs=pl.BlockSpec((None, SEQ, 1024), lambda i: (i, 0, 0)),
        out_shape=jax.ShapeDtypeStruct((b, SEQ, 1024), BF16),
        scratch_shapes=[
            pltpu.VMEM((2 * SWA_KV_HEADS, SEQ, LANES), BF16),
            pltpu.VMEM((2 * SWA_KV_HEADS, N_META, LANES), BF16),
            pltpu.VMEM((SWA_KV_HEADS, SWA_VROWS, KEYS_EXT), BF16),
        ],
        compiler_params=_cparams(1),
        name="swa_attn",
    )(sink, proj3, proj3, proj3, proj_lead, proj_lead)


DIFF_VROWS = 2 * HEAD_DIM + BF16_ROWS
DIFF_KCHUNK = 512
DIFF_PART = 512


def _diff_kernel(lamv_ref, gain_ref, q_ref, k_ref, v_ref, km_ref, vm_ref, o_ref, kx_ref, vt_ref,
                 s0_ref, s1_ref, e0_ref, e1_ref, *, lambda_init, tq):
    s_refs, e_refs = (s0_ref, s1_ref), (e0_ref, e1_ref)

    @pl.when(pl.program_id(2) == 0)
    def _():
        lo = lax.broadcasted_iota(jnp.int32, (1, LANES), 1) < HEAD_DIM
        zero = jnp.zeros((), BF16)

        def put_k(kk, sl):
            kx_ref[0, sl, :] = jnp.where(lo, kk, zero)
            kx_ref[1, sl, :] = jnp.where(lo, zero, kk)
        for c in range(SEQ // 512):
            put_k(k_ref[c * 512:(c + 1) * 512, :], slice(c * 512, (c + 1) * 512))
            for cc in range(2):
                sl = slice(c * 512 + cc * 256, c * 512 + (cc + 1) * 256)
                vt_ref[0:2 * HEAD_DIM, sl] = v_ref[sl, :].astype(F32).T.astype(BF16)
        put_k(km_ref[META_PAD:BLOCK, :], slice(SEQ, KEYS_VALID))
        vt_ref[0:2 * HEAD_DIM, SEQ:KEYS_EXT] = vm_ref[...].astype(F32).T.astype(BF16)
        vt_ref[2 * HEAD_DIM:DIFF_VROWS, :] = jnp.ones((BF16_ROWS, KEYS_EXT), BF16)
        for e_ref in e_refs:
            e_ref[SEQ:SEQ + META_PAD, :] = jnp.zeros((META_PAD, 2 * DIFF_PART), BF16)

    lv = lamv_ref[...]
    lam = (jnp.exp(jnp.sum(lv[0:1] * lv[1:2], axis=-1, keepdims=True))
           - jnp.exp(jnp.sum(lv[2:3] * lv[3:4], axis=-1, keepdims=True)) + lambda_init)
    chunks = [(k0, DIFF_KCHUNK if k0 + DIFF_KCHUNK < SEQ else KEYS_VALID - k0) for k0 in range(0, SEQ, DIFF_KCHUNK)]
    qrows = [slice(p * DIFF_PART, (p + 1) * DIFF_PART) for p in range(tq // DIFF_PART)]
    qcols = [slice((p % 2) * DIFF_PART, (p % 2 + 1) * DIFF_PART) for p in range(tq // DIFF_PART)]

    t = pl.program_id(1) * pl.num_programs(2) + pl.program_id(2)
    half_st = t & 1
    half_ld = t - ((t >> 1) << 1)

    acc, mx, out = {}, {}, {}

    def scores(it, k0, kn):
        qh, mp = it
        s = _nt_dot(kx_ref[mp, k0:k0 + kn, :], q_ref[qrows[qh], :])
        s_refs[mp][half_st, k0:k0 + kn, qcols[qh]] = s
        a = acc.setdefault(it, [None, None])
        for i, r in enumerate(range(0, kn, F32_ROWS)):
            x = s[r:r + F32_ROWS, :]
            a[i % 2] = x if a[i % 2] is None else jnp.maximum(a[i % 2], x)

    def col_max(it):
        m8 = jnp.maximum(*acc.pop(it))
        mx[it] = jnp.broadcast_to(jnp.max(m8, axis=0, keepdims=True), (BF16_ROWS, m8.shape[1]))

    def exps(it, k0, kn):
        qh, mp = it
        for r in range(k0, k0 + kn, BF16_ROWS):
            x = s_refs[mp][half_ld, r:r + BF16_ROWS, qcols[qh]]
            er = r if r < SEQ else r + META_PAD
            e_refs[mp][er:er + BF16_ROWS, qcols[qh]] = jnp.exp2((x - mx[it]).astype(BF16))

    def values(it):
        qh, mp = it
        out[it] = jnp.dot(vt_ref[...], e_refs[mp][:, qcols[qh]], preferred_element_type=F32)

    def finish(qh):
        d = 2 * HEAD_DIM
        o1, o2 = out.pop((qh, 0)), out.pop((qh, 1))
        r1 = 1.0 / o1[d:d + 1]
        r2 = lam / o2[d:d + 1]
        o = (o1[0:d] * r1 - o2[0:d] * r2).T
        y = o * lax.rsqrt(jnp.mean(o * o, axis=-1, keepdims=True) + RMS_EPS) * gain_ref[...]
        o_ref[qrows[qh], :] = (y * (1.0 - lambda_init)).astype(BF16)

    def staged(score_it=None, exp_it=None):
        if exp_it is not None:
            col_max(exp_it)
        for ch in chunks:
            if score_it is not None:
                scores(score_it, *ch)
            if exp_it is not None:
                exps(exp_it, *ch)

    items = [(p, mp) for p in range(len(qrows)) for mp in range(2)]
    for n in range(len(items) + 1):
        staged(score_it=items[n] if n < len(items) else None, exp_it=items[n - 1] if n > 0 else None)
        if n > 0:
            values(items[n - 1])
            if items[n - 1][1] == 1:
                finish(items[n - 1][0])


def _diff(lamv, gain, proj3, proj_lead, *, lambda_init, tq):
    b = proj3.shape[0]
    qb, kb, vb = QB_OFF // LANES, KB_OFF // LANES, VB_OFF // LANES
    kern = functools.partial(_diff_kernel, lambda_init=lambda_init, tq=tq)
    return pl.pallas_call(
        kern,
        grid=(b, DIFF_HEADS, SEQ // tq),
        in_specs=[
            pl.BlockSpec((4, HEAD_DIM), lambda i, h, t: (0, 0)),
            pl.BlockSpec((1, LANES), lambda i, h, t: (0, 0)),
            pl.BlockSpec((None, tq, LANES), lambda i, h, t: (i, t, qb + h)),
            pl.BlockSpec((None, SEQ, LANES), lambda i, h, t: (i, 0, kb + h)),
            pl.BlockSpec((None, SEQ, LANES), lambda i, h, t: (i, 0, vb + h)),
            pl.BlockSpec((BLOCK, LANES), lambda i, h, t: (0, kb + h)),
            pl.BlockSpec((BLOCK, LANES), lambda i, h, t: (0, vb + h)),
        ],
        out_specs=pl.BlockSpec((None, tq, LANES), lambda i, h, t: (i, t, h)),
        out_shape=jax.ShapeDtypeStruct((b, SEQ, DIFF_HEADS * LANES), BF16),
        scratch_shapes=[
            pltpu.VMEM((2, KEYS_VALID, LANES), BF16),
            pltpu.VMEM((DIFF_VROWS, KEYS_EXT), BF16),
            pltpu.VMEM((2, KEYS_VALID, 2 * DIFF_PART), F32),
            pltpu.VMEM((2, KEYS_VALID, 2 * DIFF_PART), F32),
            pltpu.VMEM((KEYS_EXT, 2 * DIFF_PART), BF16),
            pltpu.VMEM((KEYS_EXT, 2 * DIFF_PART), BF16),
        ],
        compiler_params=_cparams(3),
        name="diff_attn",
    )(lamv, gain, proj3, proj3, proj3, proj_lead, proj_lead)


def _merge_kernel(x_ref, os_ref, od_ref, gs_ref, gd_ref, ws_ref, wd_ref, wo_ref, gain_ref, o_ref):
    for r0 in range(0, x_ref.shape[0], ROWS_PER_PASS):
        rows = slice(r0, r0 + ROWS_PER_PASS)
        a = jnp.dot(os_ref[rows, :], ws_ref[...], preferred_element_type=F32)
        b = jnp.dot(od_ref[rows, :], wd_ref[...], preferred_element_type=F32)
        merged = gs_ref[rows, :].astype(F32) * a + gd_ref[rows, :].astype(F32) * b
        y = jnp.dot(merged.astype(BF16), wo_ref[...], preferred_element_type=F32)
        o_ref[rows, :] = (x_ref[rows, :]
                          + (y * lax.rsqrt(jnp.mean(y * y, axis=-1, keepdims=True) + RMS_EPS)) * gain_ref[...])


def _merge(x2d, o_swa, o_diff, proj, ws, wd, wo, gain, *, tm):
    n = x2d.shape[0]
    wspec = _resident((D_MODEL, D_MODEL))
    return pl.pallas_call(
        _merge_kernel,
        grid=(n // tm,),
        in_specs=[
            pl.BlockSpec((tm, D_MODEL), lambda i: (i, 0)),
            pl.BlockSpec((tm, D_MODEL), lambda i: (i, 0)),
            pl.BlockSpec((tm, D_MODEL), lambda i: (i, 0)),
            pl.BlockSpec((tm, D_MODEL), lambda i: (i, G_OFF // D_MODEL)),
            pl.BlockSpec((tm, D_MODEL), lambda i: (i, G_OFF // D_MODEL + 1)),
            wspec, wspec, wspec,
            _resident((1, D_MODEL)),
        ],
        out_specs=pl.BlockSpec((tm, D_MODEL), lambda i: (i, 0)),
        out_shape=jax.ShapeDtypeStruct((n, D_MODEL), F32),
        compiler_params=_cparams(1),
        name="merge_out",
    )(x2d, o_swa, o_diff, proj, proj, ws, wd, wo, gain)


FF_CHUNK = 256


def _ffn_kernel(h_ref, g1_ref, wi_ref, wo_ref, g2_ref, o_ref, f_ref):
    for r0 in range(0, h_ref.shape[0], ROWS_PER_PASS):
        rows = slice(r0, r0 + ROWS_PER_PASS)
        h = h_ref[rows, :]
        u = ((h * lax.rsqrt(jnp.mean(h * h, axis=-1, keepdims=True) + RMS_EPS)) * g1_ref[...]).astype(BF16)
        for c in range(D_FF // FF_CHUNK):
            gate = jnp.dot(u, wi_ref[:, c * FF_CHUNK:(c + 1) * FF_CHUNK], preferred_element_type=F32)
            up = jnp.dot(u, wi_ref[:, D_FF + c * FF_CHUNK:D_FF + (c + 1) * FF_CHUNK], preferred_element_type=F32)
            f_ref[rows, c * FF_CHUNK:(c + 1) * FF_CHUNK] = (jax.nn.silu(gate) * up).astype(BF16)
        f = jnp.dot(f_ref[rows, :], wo_ref[...], preferred_element_type=F32)
        o_ref[rows, :] = h + (f * lax.rsqrt(jnp.mean(f * f, axis=-1, keepdims=True) + RMS_EPS)) * g2_ref[...]


def _ffn(h2d, g1, wi, wo, g2, *, tm):
    n = h2d.shape[0]
    return pl.pallas_call(
        _ffn_kernel,
        grid=(n // tm,),
        in_specs=[
            pl.BlockSpec((tm, D_MODEL), lambda i: (i, 0)),
            _resident((1, D_MODEL)),
            _resident((D_MODEL, 2 * D_FF)),
            _resident((D_FF, D_MODEL)),
            _resident((1, D_MODEL)),
        ],
        out_specs=pl.BlockSpec((tm, D_MODEL), lambda i: (i, 0)),
        out_shape=jax.ShapeDtypeStruct((n, D_MODEL), F32),
        scratch_shapes=[pltpu.VMEM((tm, D_FF), BF16)],
        compiler_params=_cparams(1),
        name="ffn",
    )(h2d, g1, wi, wo, g2)


def _permute_cols(a):
    qa, ka, va, rest = a[..., :1024], a[..., 1024:1280], a[..., 1280:1536], a[..., 1536:]
    return jnp.concatenate([qa, rest, ka, va], axis=-1)


def kernel(x, meta_tokens, pre_mix_gain, w_in, b_gate, attn_sink, lambda_q1, lambda_k1, lambda_q2, lambda_k2,
           diff_subln_gain, w_branch_swa, w_branch_diff, w_out, post_mix_gain, pre_ffn_gain, w_ffn_in, w_ffn_out,
           post_ffn_gain):
    bsz = x.shape[0]
    if w_in.shape[0] != 1:
        raise NotImplementedError("DEPTH > 1 is not supported")
    l = 0
    lambda_init = 0.8 - 0.6 * math.exp(-0.3 * l)
    row = lambda a: a.reshape(1, -1).astype(F32)

    x2d = x.reshape(bsz * SEQ, D_MODEL)
    lead = jnp.concatenate([jnp.zeros((META_PAD, D_MODEL), x.dtype), meta_tokens.astype(x.dtype)], axis=0)
    real_tables = _rope_tables(jnp.arange(SEQ, dtype=jnp.int32) + N_META)
    lead_tables = _rope_tables(jnp.arange(BLOCK, dtype=jnp.int32) - META_PAD)

    w = _permute_cols(w_in[l]).astype(BF16)
    proj = _inproj(x2d, row(pre_mix_gain[l]), w, row(b_gate[l]), *real_tables, tm=TOKENS_PER_STEP)
    proj_lead = _inproj(lead, row(pre_mix_gain[l]), w, row(b_gate[l]), *lead_tables, tm=BLOCK)
    proj3 = proj.reshape(bsz, SEQ, IN_COLS)

    o_swa = _swa(row(attn_sink[l]), proj3, proj_lead)
    lamv = jnp.stack([lambda_q1[l], lambda_k1[l], lambda_q2[l], lambda_k2[l]]).astype(F32)
    o_diff = _diff(lamv, row(diff_subln_gain[l]), proj3, proj_lead, lambda_init=lambda_init,
                   tq=DIFF_QUERIES_PER_STEP)

    h2d = _merge(x2d, o_swa.reshape(bsz * SEQ, D_MODEL), o_diff.reshape(bsz * SEQ, D_MODEL), proj,
                 w_branch_swa[l].astype(BF16), w_branch_diff[l].astype(BF16), w_out[l].astype(BF16),
                 row(post_mix_gain[l]), tm=TOKENS_PER_STEP)
    h2d = _ffn(h2d, row(pre_ffn_gain[l]), w_ffn_in[l].astype(BF16), w_ffn_out[l].astype(BF16),
               row(post_ffn_gain[l]), tm=TOKENS_PER_STEP)
    return h2d.reshape(bsz, SEQ, D_MODEL)
```

```python
import functools
import math

import jax
import jax.numpy as jnp
from jax import lax
from jax.experimental import pallas as pl
from jax.experimental.pallas import tpu as pltpu

F32 = jnp.float32
BF16 = jnp.bfloat16

D_MODEL = 1024
SEQ = 2048
HEAD_DIM = 64
HALF = HEAD_DIM // 2
BLOCK = 128
N_META = 16
META_PAD = BLOCK - N_META
WINDOW = 128
ROPE_THETA = 10000.0
RMS_EPS = 1e-6
NEG_INF = -1e30
SWA_Q_HEADS = 16
SWA_KV_HEADS = 4
DIFF_HEADS = 8
D_FF = 2816
LANES = 128
F32_ROWS = 8
BF16_ROWS = 16

QA_OFF, QB_OFF, KB_OFF, VB_OFF, G_OFF, KA_OFF, VA_OFF = 0, 1024, 2048, 3072, 4096, 6144, 6400
IN_COLS = 6656
PROJ_GROUP = 512
BAND = 3 * BLOCK
KEYS_EXT = SEQ + BLOCK
KEYS_VALID = SEQ + N_META

LOG2E = math.log2(math.e)
Q_SCALE = HEAD_DIM ** -0.5 * LOG2E

VMEM_LIMIT = 56 * 1024 * 1024

TOKENS_PER_STEP = 1024
ROWS_PER_PASS = 512
DIFF_QUERIES_PER_STEP = SEQ


def _cparams(n_axes):
    return pltpu.CompilerParams(dimension_semantics=("arbitrary",) * n_axes, vmem_limit_bytes=VMEM_LIMIT)


def _nt_dot(a, b):
    return lax.dot_general(a, b, (((1,), (1,)), ((), ())), preferred_element_type=F32)


def _resident(shape):
    return pl.BlockSpec(shape, lambda *_: (0,) * len(shape), pipeline_mode=pl.Buffered(1))


def _col_kind(col):
    if col < KB_OFF:
        return "rope", Q_SCALE
    if col < VB_OFF:
        return "rope", 1.0
    if col < G_OFF:
        return "plain", 1.0
    if col < KA_OFF:
        return "gate", 1.0
    if col < VA_OFF:
        return "rope", 1.0
    return "plain", 1.0


def _inproj_kernel(x_ref, g_ref, w_ref, b_ref, cos_ref, sa_ref, sb_ref, o_ref):
    tm = x_ref.shape[0]
    for r0 in range(0, tm, min(tm, ROWS_PER_PASS)):
        rows = slice(r0, r0 + min(tm, ROWS_PER_PASS))
        xx = x_ref[rows, :]
        ms = jnp.mean(xx * xx, axis=-1, keepdims=True)
        u = ((xx * lax.rsqrt(ms + RMS_EPS)) * g_ref[...]).astype(BF16)
        cos, sa, sb = cos_ref[rows, :], sa_ref[rows, :], sb_ref[rows, :]
        for grp in range(IN_COLS // PROJ_GROUP):
            c0 = grp * PROJ_GROUP
            acc = jnp.dot(u, w_ref[:, c0:c0 + PROJ_GROUP], preferred_element_type=F32)
            for ci in range(PROJ_GROUP // LANES):
                col = c0 + ci * LANES
                t = acc[:, ci * LANES:(ci + 1) * LANES]
                kind, scale = _col_kind(col)
                if kind == "rope":
                    t = t * cos + pltpu.roll(t, LANES - HALF, 1) * sa + pltpu.roll(t, HALF, 1) * sb
                    if scale != 1.0:
                        t = t * scale
                elif kind == "gate":
                    t = jax.nn.sigmoid(t + b_ref[:, col - G_OFF:col - G_OFF + LANES])
                o_ref[rows, col:col + LANES] = t.astype(BF16)


def _inproj(x2d, gain, w, b_gate, cos, sa, sb, *, tm):
    n = x2d.shape[0]
    nt = cos.shape[0] // tm
    table = pl.BlockSpec((tm, LANES), lambda i: (i % nt, 0))
    return pl.pallas_call(
        _inproj_kernel,
        grid=(n // tm,),
        in_specs=[
            pl.BlockSpec((tm, D_MODEL), lambda i: (i, 0)),
            _resident((1, D_MODEL)),
            _resident((D_MODEL, IN_COLS)),
            _resident((1, 2 * D_MODEL)),
            table, table, table,
        ],
        out_specs=pl.BlockSpec((tm, IN_COLS), lambda i: (i, 0)),
        out_shape=jax.ShapeDtypeStruct((n, IN_COLS), BF16),
        compiler_params=_cparams(1),
        name="inproj",
    )(x2d, gain, w, b_gate, cos, sa, sb)


def _rope_tables(pos):
    inv_freq = ROPE_THETA ** (-jnp.arange(0, HEAD_DIM, 2, dtype=F32) / HEAD_DIM)
    ang = pos.astype(F32)[:, None] * inv_freq[None, :]
    lane = jnp.arange(LANES)
    cos = jnp.cos(ang)[:, lane % HALF]
    sin = jnp.sin(ang)[:, lane % HALF]
    first = (lane % HEAD_DIM) < HALF
    return cos, jnp.where(first, -sin, 0.0), jnp.where(first, 0.0, sin)


SWA_VROWS = HEAD_DIM + BF16_ROWS
SWA_KROWS = BAND + N_META
SWA_UNROLL = 4
SWA_VALUE_LAG = 5


def _swa_kernel(sink_ref, q_ref, k_ref, v_ref, km_ref, vm_ref, o_ref, kx_ref, kmx_ref, vt_ref):
    lo = lax.broadcasted_iota(jnp.int32, (1, LANES), 1) < HEAD_DIM

    def expand_k(src, dst, sl_src, sl_dst):
        for lb in range(2):
            blk = src[sl_src, lb * LANES:(lb + 1) * LANES].astype(F32)
            rolled = pltpu.roll(blk, HEAD_DIM, 1)
            g0, g1 = 2 * lb, 2 * lb + 1
            dst[2 * g0, sl_dst, :] = jnp.where(lo, blk, 0.0).astype(BF16)
            dst[2 * g0 + 1, sl_dst, :] = jnp.where(lo, 0.0, rolled).astype(BF16)
            dst[2 * g1, sl_dst, :] = jnp.where(lo, rolled, 0.0).astype(BF16)
            dst[2 * g1 + 1, sl_dst, :] = jnp.where(lo, 0.0, blk).astype(BF16)

    def put_vt(src, sl_src, sl_dst):
        vt = src[sl_src, :].astype(F32).T
        for g in range(SWA_KV_HEADS):
            vt_ref[g, 0:HEAD_DIM, sl_dst] = vt[g * HEAD_DIM:(g + 1) * HEAD_DIM].astype(BF16)

    def prep_body(c, carry):
        sl = pl.ds(pl.multiple_of(c * 256, 256), 256)
        expand_k(k_ref, kx_ref, sl, sl)
        put_vt(v_ref, sl, sl)
        return carry
    lax.fori_loop(0, SEQ // 256, prep_body, 0)
    expand_k(km_ref, kmx_ref, slice(META_PAD, BLOCK), slice(None))
    put_vt(vm_ref, slice(None), slice(SEQ, KEYS_EXT))
    vt_ref[:, HEAD_DIM:SWA_VROWS, :] = jnp.ones((SWA_KV_HEADS, BF16_ROWS, KEYS_EXT), BF16)

    krow = lax.broadcasted_iota(jnp.int32, (SWA_KROWS, 2 * LANES), 0)
    kq_delta = krow - lax.broadcasted_iota(jnp.int32, (SWA_KROWS, 2 * LANES), 1) % LANES
    first_blk = lax.broadcasted_iota(jnp.int32, (1, 2 * LANES), 1) < LANES

    filler_rows = jnp.zeros((META_PAD, 2 * LANES), BF16)

    def body(j, carry):
        q0, start, bias = [], [], []
        for i in range(SWA_UNROLL):
            r = j * SWA_UNROLL + i
            q0.append(pl.multiple_of(r * BLOCK, BLOCK))
            start.append(pl.multiple_of(jnp.clip(r * BLOCK - BLOCK, 0, SEQ - BAND), BLOCK))
            visible = (krow >= BAND) | (jnp.abs(kq_delta + (start[i] - q0[i])) <= WINDOW)
            bias.append(jnp.where(visible, 0.0, NEG_INF).astype(F32))

        chains = [(i, g, v) for i in range(SWA_UNROLL) for g in range(SWA_KV_HEADS) for v in range(2)]

        def scores(i, g, v):
            lb0, lb1 = 2 * g, 2 * g + 1
            qq = jnp.concatenate([q_ref[pl.ds(q0[i], BLOCK), lb0 * LANES:(lb0 + 1) * LANES],
                                  q_ref[pl.ds(q0[i], BLOCK), lb1 * LANES:(lb1 + 1) * LANES]], axis=0)
            kcat = jnp.concatenate([kx_ref[2 * g + v, pl.ds(start[i], BAND), :], kmx_ref[2 * g + v]], axis=0)
            sink = jnp.where(first_blk, sink_ref[0, 2 * lb0 + v], sink_ref[0, 2 * lb1 + v]) * LOG2E
            return _nt_dot(kcat, qq), sink, i

        def masked_max(s, sink, i):
            s = s + bias[i]
            return s, jnp.maximum(jnp.max(s, axis=0, keepdims=True), sink), sink

        def probs(s, m, sink):
            p = jnp.exp2((s - m).astype(BF16))
            return jnp.concatenate([p[0:BAND], filler_rows, p[BAND:SWA_KROWS]], axis=0), jnp.exp2(sink - m)

        def values(i, g, pt, psink):
            vcat = jnp.concatenate([vt_ref[g, :, pl.ds(start[i], BAND)], vt_ref[g, :, SEQ:KEYS_EXT]], axis=1)
            o = jnp.dot(vcat, pt, preferred_element_type=F32)
            return o[0:HEAD_DIM] * (1.0 / (o[HEAD_DIM:HEAD_DIM + 1] + psink))

        st_a, st_b, st_c, st_d = {}, {}, {}, {}
        for step in range(len(chains) + SWA_VALUE_LAG):
            if step < len(chains):
                st_a[step] = scores(*chains[step])
            if 0 <= step - 1 < len(chains):
                st_b[step - 1] = masked_max(*st_a.pop(step - 1))
            if 0 <= step - 2 < len(chains):
                st_c[step - 2] = probs(*st_b.pop(step - 2))
            if 0 <= step - SWA_VALUE_LAG < len(chains):
                i, g, v = chains[step - SWA_VALUE_LAG]
                st_d[v] = values(i, g, *st_c.pop(step - SWA_VALUE_LAG))
                if v == 1:
                    ot = jnp.concatenate([st_d.pop(0), st_d.pop(1)], axis=0)
                    for half, lb in enumerate((2 * g, 2 * g + 1)):
                        o_ref[pl.ds(q0[i], BLOCK), lb * LANES:(lb + 1) * LANES] = (
                            ot[:, half * LANES:(half + 1) * LANES].T.astype(BF16))
        return carry
    lax.fori_loop(0, SEQ // (BLOCK * SWA_UNROLL), body, 0)


def _swa(sink, proj3, proj_lead):
    b = proj3.shape[0]
    return pl.pallas_call(
        _swa_kernel,
        grid=(b,),
        in_specs=[
            pl.BlockSpec(memory_space=pltpu.SMEM),
            pl.BlockSpec((None, SEQ, 1024), lambda i: (i, 0, QA_OFF // 1024)),
            pl.BlockSpec((None, SEQ, 256), lambda i: (i, 0, KA_OFF // 256)),
            pl.BlockSpec((None, SEQ, 256), lambda i: (i, 0, VA_OFF // 256)),
            pl.BlockSpec((BLOCK, 256), lambda i: (0, KA_OFF // 256)),
            pl.BlockSpec((BLOCK, 256), lambda i: (0, VA_OFF // 256)),
        ],
        out_specs=pl.BlockSpec((None, SEQ, 1024), lambda i: (i, 0, 0)),
        out_shape=jax.ShapeDtypeStruct((b, SEQ, 1024), BF16),
        scratch_shapes=[
            pltpu.VMEM((2 * SWA_KV_HEADS, SEQ, LANES), BF16),
            pltpu.VMEM((2 * SWA_KV_HEADS, N_META, LANES), BF16),
            pltpu.VMEM((SWA_KV_HEADS, SWA_VROWS, KEYS_EXT), BF16),
        ],
        compiler_params=_cparams(1),
        name="swa_attn",
    )(sink, proj3, proj3, proj3, proj_lead, proj_lead)


DIFF_VROWS = 2 * HEAD_DIM + BF16_ROWS
DIFF_KCHUNK = 512
DIFF_PART = 512


def _diff_kernel(lamv_ref, gain_ref, q_ref, k_ref, v_ref, km_ref, vm_ref, o_ref, kx_ref, vt_ref,
                 s0_ref, s1_ref, e0_ref, e1_ref, *, lambda_init, tq):
    s_refs, e_refs = (s0_ref, s1_ref), (e0_ref, e1_ref)

    @pl.when(pl.program_id(2) == 0)
    def _():
        lo = lax.broadcasted_iota(jnp.int32, (1, LANES), 1) < HEAD_DIM
        zero = jnp.zeros((), BF16)

        def put_k(kk, sl):
            kx_ref[0, sl, :] = jnp.where(lo, kk, zero)
            kx_ref[1, sl, :] = jnp.where(lo, zero, kk)
        for c in range(SEQ // 512):
            put_k(k_ref[c * 512:(c + 1) * 512, :], slice(c * 512, (c + 1) * 512))
            for cc in range(2):
                sl = slice(c * 512 + cc * 256, c * 512 + (cc + 1) * 256)
                vt_ref[0:2 * HEAD_DIM, sl] = v_ref[sl, :].astype(F32).T.astype(BF16)
        put_k(km_ref[META_PAD:BLOCK, :], slice(SEQ, KEYS_VALID))
        vt_ref[0:2 * HEAD_DIM, SEQ:KEYS_EXT] = vm_ref[...].astype(F32).T.astype(BF16)
        vt_ref[2 * HEAD_DIM:DIFF_VROWS, :] = jnp.ones((BF16_ROWS, KEYS_EXT), BF16)
        for e_ref in e_refs:
            e_ref[SEQ:SEQ + META_PAD, :] = jnp.zeros((META_PAD, 2 * DIFF_PART), BF16)

    lv = lamv_ref[...]
    lam = (jnp.exp(jnp.sum(lv[0:1] * lv[1:2], axis=-1, keepdims=True))
           - jnp.exp(jnp.sum(lv[2:3] * lv[3:4], axis=-1, keepdims=True)) + lambda_init)
    chunks = [(k0, DIFF_KCHUNK if k0 + DIFF_KCHUNK < SEQ else KEYS_VALID - k0) for k0 in range(0, SEQ, DIFF_KCHUNK)]
    qrows = [slice(p * DIFF_PART, (p + 1) * DIFF_PART) for p in range(tq // DIFF_PART)]
    qcols = [slice((p % 2) * DIFF_PART, (p % 2 + 1) * DIFF_PART) for p in range(tq // DIFF_PART)]

    t = pl.program_id(1) * pl.num_programs(2) + pl.program_id(2)
    half_st = t & 1
    half_ld = t - ((t >> 1) << 1)

    acc, mx, out = {}, {}, {}

    def scores(it, k0, kn):
        qh, mp = it
        s = _nt_dot(kx_ref[mp, k0:k0 + kn, :], q_ref[qrows[qh], :])
        s_refs[mp][half_st, k0:k0 + kn, qcols[qh]] = s
        a = acc.setdefault(it, [None, None])
        for i, r in enumerate(range(0, kn, F32_ROWS)):
            x = s[r:r + F32_ROWS, :]
            a[i % 2] = x if a[i % 2] is None else jnp.maximum(a[i % 2], x)

    def col_max(it):
        m8 = jnp.maximum(*acc.pop(it))
        mx[it] = jnp.broadcast_to(jnp.max(m8, axis=0, keepdims=True), (BF16_ROWS, m8.shape[1]))

    def exps(it, k0, kn):
        qh, mp = it
        for r in range(k0, k0 + kn, BF16_ROWS):
            x = s_refs[mp][half_ld, r:r + BF16_ROWS, qcols[qh]]
            er = r if r < SEQ else r + META_PAD
            e_refs[mp][er:er + BF16_ROWS, qcols[qh]] = jnp.exp2((x - mx[it]).astype(BF16))

    def values(it):
        qh, mp = it
        out[it] = jnp.dot(vt_ref[...], e_refs[mp][:, qcols[qh]], preferred_element_type=F32)

    def finish(qh):
        d = 2 * HEAD_DIM
        o1, o2 = out.pop((qh, 0)), out.pop((qh, 1))
        r1 = 1.0 / o1[d:d + 1]
        r2 = lam / o2[d:d + 1]
        o = (o1[0:d] * r1 - o2[0:d] * r2).T
        y = o * lax.rsqrt(jnp.mean(o * o, axis=-1, keepdims=True) + RMS_EPS) * gain_ref[...]
        o_ref[qrows[qh], :] = (y * (1.0 - lambda_init)).astype(BF16)

    def staged(score_it=None, exp_it=None):
        if exp_it is not None:
            col_max(exp_it)
        for ch in chunks:
            if score_it is not None:
                scores(score_it, *ch)
            if exp_it is not None:
                exps(exp_it, *ch)

    items = [(p, mp) for p in range(len(qrows)) for mp in range(2)]
    for n in range(len(items) + 1):
        staged(score_it=items[n] if n < len(items) else None, exp_it=items[n - 1] if n > 0 else None)
        if n > 0:
            values(items[n - 1])
            if items[n - 1][1] == 1:
                finish(items[n - 1][0])


def _diff(lamv, gain, proj3, proj_lead, *, lambda_init, tq):
    b = proj3.shape[0]
    qb, kb, vb = QB_OFF // LANES, KB_OFF // LANES, VB_OFF // LANES
    kern = functools.partial(_diff_kernel, lambda_init=lambda_init, tq=tq)
    return pl.pallas_call(
        kern,
        grid=(b, DIFF_HEADS, SEQ // tq),
        in_specs=[
            pl.BlockSpec((4, HEAD_DIM), lambda i, h, t: (0, 0)),
            pl.BlockSpec((1, LANES), lambda i, h, t: (0, 0)),
            pl.BlockSpec((None, tq, LANES), lambda i, h, t: (i, t, qb + h)),
            pl.BlockSpec((None, SEQ, LANES), lambda i, h, t: (i, 0, kb + h)),
            pl.BlockSpec((None, SEQ, LANES), lambda i, h, t: (i, 0, vb + h)),
            pl.BlockSpec((BLOCK, LANES), lambda i, h, t: (0, kb + h)),
            pl.BlockSpec((BLOCK, LANES), lambda i, h, t: (0, vb + h)),
        ],
        out_specs=pl.BlockSpec((None, tq, LANES), lambda i, h, t: (i, t, h)),
        out_shape=jax.ShapeDtypeStruct((b, SEQ, DIFF_HEADS * LANES), BF16),
        scratch_shapes=[
            pltpu.VMEM((2, KEYS_VALID, LANES), BF16),
            pltpu.VMEM((DIFF_VROWS, KEYS_EXT), BF16),
            pltpu.VMEM((2, KEYS_VALID, 2 * DIFF_PART), F32),
            pltpu.VMEM((2, KEYS_VALID, 2 * DIFF_PART), F32),
            pltpu.VMEM((KEYS_EXT, 2 * DIFF_PART), BF16),
            pltpu.VMEM((KEYS_EXT, 2 * DIFF_PART), BF16),
        ],
        compiler_params=_cparams(3),
        name="diff_attn",
    )(lamv, gain, proj3, proj3, proj3, proj_lead, proj_lead)


def _merge_kernel(x_ref, os_ref, od_ref, gs_ref, gd_ref, ws_ref, wd_ref, wo_ref, gain_ref, o_ref):
    for r0 in range(0, x_ref.shape[0], ROWS_PER_PASS):
        rows = slice(r0, r0 + ROWS_PER_PASS)
        a = jnp.dot(os_ref[rows, :], ws_ref[...], preferred_element_type=F32)
        b = jnp.dot(od_ref[rows, :], wd_ref[...], preferred_element_type=F32)
        merged = gs_ref[rows, :].astype(F32) * a + gd_ref[rows, :].astype(F32) * b
        y = jnp.dot(merged.astype(BF16), wo_ref[...], preferred_element_type=F32)
        o_ref[rows, :] = (x_ref[rows, :]
                          + (y * lax.rsqrt(jnp.mean(y * y, axis=-1, keepdims=True) + RMS_EPS)) * gain_ref[...])


def _merge(x2d, o_swa, o_diff, proj, ws, wd, wo, gain, *, tm):
    n = x2d.shape[0]
    wspec = _resident((D_MODEL, D_MODEL))
    return pl.pallas_call(
        _merge_kernel,
        grid=(n // tm,),
        in_specs=[
            pl.BlockSpec((tm, D_MODEL), lambda i: (i, 0)),
            pl.BlockSpec((tm, D_MODEL), lambda i: (i, 0)),
            pl.BlockSpec((tm, D_MODEL), lambda i: (i, 0)),
            pl.BlockSpec((tm, D_MODEL), lambda i: (i, G_OFF // D_MODEL)),
            pl.BlockSpec((tm, D_MODEL), lambda i: (i, G_OFF // D_MODEL + 1)),
            wspec, wspec, wspec,
            _resident((1, D_MODEL)),
        ],
        out_specs=pl.BlockSpec((tm, D_MODEL), lambda i: (i, 0)),
        out_shape=jax.ShapeDtypeStruct((n, D_MODEL), F32),
        compiler_params=_cparams(1),
        name="merge_out",
    )(x2d, o_swa, o_diff, proj, proj, ws, wd, wo, gain)


FF_CHUNK = 256


def _ffn_kernel(h_ref, g1_ref, wi_ref, wo_ref, g2_ref, o_ref, f_ref):
    for r0 in range(0, h_ref.shape[0], ROWS_PER_PASS):
        rows = slice(r0, r0 + ROWS_PER_PASS)
        h = h_ref[rows, :]
        u = ((h * lax.rsqrt(jnp.mean(h * h, axis=-1, keepdims=True) + RMS_EPS)) * g1_ref[...]).astype(BF16)
        for c in range(D_FF // FF_CHUNK):
            gate = jnp.dot(u, wi_ref[:, c * FF_CHUNK:(c + 1) * FF_CHUNK], preferred_element_type=F32)
            up = jnp.dot(u, wi_ref[:, D_FF + c * FF_CHUNK:D_FF + (c + 1) * FF_CHUNK], preferred_element_type=F32)
            f_ref[rows, c * FF_CHUNK:(c + 1) * FF_CHUNK] = (jax.nn.silu(gate) * up).astype(BF16)
        f = jnp.dot(f_ref[rows, :], wo_ref[...], preferred_element_type=F32)
        o_ref[rows, :] = h + (f * lax.rsqrt(jnp.mean(f * f, axis=-1, keepdims=True) + RMS_EPS)) * g2_ref[...]


def _ffn(h2d, g1, wi, wo, g2, *, tm):
    n = h2d.shape[0]
    return pl.pallas_call(
        _ffn_kernel,
        grid=(n // tm,),
        in_specs=[
            pl.BlockSpec((tm, D_MODEL), lambda i: (i, 0)),
            _resident((1, D_MODEL)),
            _resident((D_MODEL, 2 * D_FF)),
            _resident((D_FF, D_MODEL)),
            _resident((1, D_MODEL)),
        ],
        out_specs=pl.BlockSpec((tm, D_MODEL), lambda i: (i, 0)),
        out_shape=jax.ShapeDtypeStruct((n, D_MODEL), F32),
        scratch_shapes=[pltpu.VMEM((tm, D_FF), BF16)],
        compiler_params=_cparams(1),
        name="ffn",
    )(h2d, g1, wi, wo, g2)


def _permute_cols(a):
    qa, ka, va, rest = a[..., :1024], a[..., 1024:1280], a[..., 1280:1536], a[..., 1536:]
    return jnp.concatenate([qa, rest, ka, va], axis=-1)


def kernel(x, meta_tokens, pre_mix_gain, w_in, b_gate, attn_sink, lambda_q1, lambda_k1, lambda_q2, lambda_k2,
           diff_subln_gain, w_branch_swa, w_branch_diff, w_out, post_mix_gain, pre_ffn_gain, w_ffn_in, w_ffn_out,
           post_ffn_gain):
    bsz = x.shape[0]
    if w_in.shape[0] != 1:
        raise NotImplementedError("DEPTH > 1 is not supported")
    l = 0
    lambda_init = 0.8 - 0.6 * math.exp(-0.3 * l)
    row = lambda a: a.reshape(1, -1).astype(F32)

    x2d = x.reshape(bsz * SEQ, D_MODEL)
    lead = jnp.concatenate([jnp.zeros((META_PAD, D_MODEL), x.dtype), meta_tokens.astype(x.dtype)], axis=0)
    real_tables = _rope_tables(jnp.arange(SEQ, dtype=jnp.int32) + N_META)
    lead_tables = _rope_tables(jnp.arange(BLOCK, dtype=jnp.int32) - META_PAD)

    w = _permute_cols(w_in[l]).astype(BF16)
    proj = _inproj(x2d, row(pre_mix_gain[l]), w, row(b_gate[l]), *real_tables, tm=TOKENS_PER_STEP)
    proj_lead = _inproj(lead, row(pre_mix_gain[l]), w, row(b_gate[l]), *lead_tables, tm=BLOCK)
    proj3 = proj.reshape(bsz, SEQ, IN_COLS)

    o_swa = _swa(row(attn_sink[l]), proj3, proj_lead)
    lamv = jnp.stack([lambda_q1[l], lambda_k1[l], lambda_q2[l], lambda_k2[l]]).astype(F32)
    o_diff = _diff(lamv, row(diff_subln_gain[l]), proj3, proj_lead, lambda_init=lambda_init,
                   tq=DIFF_QUERIES_PER_STEP)

    h2d = _merge(x2d, o_swa.reshape(bsz * SEQ, D_MODEL), o_diff.reshape(bsz * SEQ, D_MODEL), proj,
                 w_branch_swa[l].astype(BF16), w_branch_diff[l].astype(BF16), w_out[l].astype(BF16),
                 row(post_mix_gain[l]), tm=TOKENS_PER_STEP)
    h2d = _ffn(h2d, row(pre_ffn_gain[l]), w_ffn_in[l].astype(BF16), w_ffn_out[l].astype(BF16),
               row(post_ffn_gain[l]), tm=TOKENS_PER_STEP)
    return h2d.reshape(bsz, SEQ, D_MODEL)
```

```python
import functools
import math

import jax
import jax.numpy as jnp
from jax import lax
from jax.experimental import pallas as pl
from jax.experimental.pallas import tpu as pltpu

F32 = jnp.float32
BF16 = jnp.bfloat16

D_MODEL = 1024
SEQ = 2048
HEAD_DIM = 64
HALF = HEAD_DIM // 2
BLOCK = 128
N_META = 16
META_PAD = BLOCK - N_META
WINDOW = 128
ROPE_THETA = 10000.0
RMS_EPS = 1e-6
NEG_INF = -1e30
SWA_Q_HEADS = 16
SWA_KV_HEADS = 4
DIFF_HEADS = 8
D_FF = 2816
LANES = 128
F32_ROWS = 8
BF16_ROWS = 16

QA_OFF, QB_OFF, KB_OFF, VB_OFF, G_OFF, KA_OFF, VA_OFF = 0, 1024, 2048, 3072, 4096, 6144, 6400
IN_COLS = 6656
PROJ_GROUP = 512
PROJ_ORDER = (0, 8, 1, 9, 2, 10, 3, 11, 4, 5, 12, 6, 7)
BAND = 3 * BLOCK
KEYS_EXT = SEQ + BLOCK
KEYS_VALID = SEQ + N_META

LOG2E = math.log2(math.e)
Q_SCALE = HEAD_DIM ** -0.5 * LOG2E

VMEM_LIMIT = 56 * 1024 * 1024

TOKENS_PER_STEP = 1024
ROWS_PER_PASS = 512
DIFF_QUERIES_PER_STEP = SEQ


def _cparams(n_axes):
    return pltpu.CompilerParams(dimension_semantics=("arbitrary",) * n_axes, vmem_limit_bytes=VMEM_LIMIT)


def _nt_dot(a, b):
    return lax.dot_general(a, b, (((1,), (1,)), ((), ())), preferred_element_type=F32)


def _resident(shape):
    return pl.BlockSpec(shape, lambda *_: (0,) * len(shape), pipeline_mode=pl.Buffered(1))


def _col_kind(col):
    if col < KB_OFF:
        return "rope", Q_SCALE
    if col < VB_OFF:
        return "rope", 1.0
    if col < G_OFF:
        return "plain", 1.0
    if col < KA_OFF:
        return "gate", 1.0
    if col < VA_OFF:
        return "rope", 1.0
    return "plain", 1.0


def _inproj_kernel(x_ref, g_ref, w_ref, b_ref, cos_ref, sa_ref, sb_ref, o_ref):
    tm = x_ref.shape[0]
    for r0 in range(0, tm, min(tm, ROWS_PER_PASS)):
        rows = slice(r0, r0 + min(tm, ROWS_PER_PASS))
        xx = x_ref[rows, :]
        ms = jnp.mean(xx * xx, axis=-1, keepdims=True)
        u = ((xx * lax.rsqrt(ms + RMS_EPS)) * g_ref[...]).astype(BF16)
        cos, sa, sb = cos_ref[rows, :], sa_ref[rows, :], sb_ref[rows, :]
        for grp in PROJ_ORDER:
            c0 = grp * PROJ_GROUP
            acc = jnp.dot(u, w_ref[:, c0:c0 + PROJ_GROUP], preferred_element_type=F32)
            for ci in range(PROJ_GROUP // LANES):
                col = c0 + ci * LANES
                t = acc[:, ci * LANES:(ci + 1) * LANES]
                kind, scale = _col_kind(col)
                if kind == "rope":
                    t = t * cos + pltpu.roll(t, LANES - HALF, 1) * sa + pltpu.roll(t, HALF, 1) * sb
                    if scale != 1.0:
                        t = t * scale
                elif kind == "gate":
                    t = jax.nn.sigmoid(t + b_ref[:, col - G_OFF:col - G_OFF + LANES])
                o_ref[rows, col:col + LANES] = t.astype(BF16)


def _inproj(x2d, gain, w, b_gate, cos, sa, sb, *, tm):
    n = x2d.shape[0]
    nt = cos.shape[0] // tm
    table = pl.BlockSpec((tm, LANES), lambda i: (i % nt, 0))
    return pl.pallas_call(
        _inproj_kernel,
        grid=(n // tm,),
        in_specs=[
            pl.BlockSpec((tm, D_MODEL), lambda i: (i, 0)),
            _resident((1, D_MODEL)),
            _resident((D_MODEL, IN_COLS)),
            _resident((1, 2 * D_MODEL)),
            table, table, table,
        ],
        out_specs=pl.BlockSpec((tm, IN_COLS), lambda i: (i, 0)),
        out_shape=jax.ShapeDtypeStruct((n, IN_COLS), BF16),
        compiler_params=_cparams(1),
        name="inproj",
    )(x2d, gain, w, b_gate, cos, sa, sb)


def _rope_tables(pos):
    inv_freq = ROPE_THETA ** (-jnp.arange(0, HEAD_DIM, 2, dtype=F32) / HEAD_DIM)
    ang = pos.astype(F32)[:, None] * inv_freq[None, :]
    lane = jnp.arange(LANES)
    cos = jnp.cos(ang)[:, lane % HALF]
    sin = jnp.sin(ang)[:, lane % HALF]
    first = (lane % HEAD_DIM) < HALF
    return cos, jnp.where(first, -sin, 0.0), jnp.where(first, 0.0, sin)


SWA_VROWS = HEAD_DIM + BF16_ROWS
SWA_KROWS = BAND + N_META
SWA_UNROLL = 4
SWA_VALUE_LAG = 5


def _swa_kernel(sink_ref, q_ref, k_ref, v_ref, km_ref, vm_ref, o_ref, kx_ref, kmx_ref, vt_ref):
    lo = lax.broadcasted_iota(jnp.int32, (1, LANES), 1) < HEAD_DIM

    def expand_k(src, dst, sl_src, sl_dst):
        for lb in range(2):
            blk = src[sl_src, lb * LANES:(lb + 1) * LANES].astype(F32)
            rolled = pltpu.roll(blk, HEAD_DIM, 1)
            g0, g1 = 2 * lb, 2 * lb + 1
            dst[2 * g0, sl_dst, :] = jnp.where(lo, blk, 0.0).astype(BF16)
            dst[2 * g0 + 1, sl_dst, :] = jnp.where(lo, 0.0, rolled).astype(BF16)
            dst[2 * g1, sl_dst, :] = jnp.where(lo, rolled, 0.0).astype(BF16)
            dst[2 * g1 + 1, sl_dst, :] = jnp.where(lo, 0.0, blk).astype(BF16)

    def put_vt(src, sl_src, sl_dst):
        vt = src[sl_src, :].astype(F32).T
        for g in range(SWA_KV_HEADS):
            vt_ref[g, 0:HEAD_DIM, sl_dst] = vt[g * HEAD_DIM:(g + 1) * HEAD_DIM].astype(BF16)

    def prep_body(c, carry):
        sl = pl.ds(pl.multiple_of(c * 256, 256), 256)
        expand_k(k_ref, kx_ref, sl, sl)
        put_vt(v_ref, sl, sl)
        return carry
    lax.fori_loop(0, SEQ // 256, prep_body, 0)
    expand_k(km_ref, kmx_ref, slice(META_PAD, BLOCK), slice(None))
    put_vt(vm_ref, slice(None), slice(SEQ, KEYS_EXT))
    vt_ref[:, HEAD_DIM:SWA_VROWS, :] = jnp.ones((SWA_KV_HEADS, BF16_ROWS, KEYS_EXT), BF16)

    krow = lax.broadcasted_iota(jnp.int32, (SWA_KROWS, 2 * LANES), 0)
    kq_delta = krow - lax.broadcasted_iota(jnp.int32, (SWA_KROWS, 2 * LANES), 1) % LANES
    first_blk = lax.broadcasted_iota(jnp.int32, (1, 2 * LANES), 1) < LANES

    filler_rows = jnp.zeros((META_PAD, 2 * LANES), BF16)

    def body(j, carry):
        q0, start, bias = [], [], []
        for i in range(SWA_UNROLL):
            r = j * SWA_UNROLL + i
            q0.append(pl.multiple_of(r * BLOCK, BLOCK))
            start.append(pl.multiple_of(jnp.clip(r * BLOCK - BLOCK, 0, SEQ - BAND), BLOCK))
            visible = (krow >= BAND) | (jnp.abs(kq_delta + (start[i] - q0[i])) <= WINDOW)
            bias.append(jnp.where(visible, 0.0, NEG_INF).astype(F32))

        chains = [(i, g, v) for i in range(SWA_UNROLL) for g in range(SWA_KV_HEADS) for v in range(2)]

        def scores(i, g, v):
            lb0, lb1 = 2 * g, 2 * g + 1
            qq = jnp.concatenate([q_ref[pl.ds(q0[i], BLOCK), lb0 * LANES:(lb0 + 1) * LANES],
                                  q_ref[pl.ds(q0[i], BLOCK), lb1 * LANES:(lb1 + 1) * LANES]], axis=0)
            kcat = jnp.concatenate([kx_ref[2 * g + v, pl.ds(start[i], BAND), :], kmx_ref[2 * g + v]], axis=0)
            sink = jnp.where(first_blk, sink_ref[0, 2 * lb0 + v], sink_ref[0, 2 * lb1 + v]) * LOG2E
            return _nt_dot(kcat, qq), sink, i

        def masked_max(s, sink, i):
            s = s + bias[i]
            return s, jnp.maximum(jnp.max(s, axis=0, keepdims=True), sink), sink

        def probs(s, m, sink):
            p = jnp.exp2((s - m).astype(BF16))
            return jnp.concatenate([p[0:BAND], filler_rows, p[BAND:SWA_KROWS]], axis=0), jnp.exp2(sink - m)

        def values(i, g, pt, psink):
            vcat = jnp.concatenate([vt_ref[g, :, pl.ds(start[i], BAND)], vt_ref[g, :, SEQ:KEYS_EXT]], axis=1)
            o = jnp.dot(vcat, pt, preferred_element_type=F32)
            return o[0:HEAD_DIM] * (1.0 / (o[HEAD_DIM:HEAD_DIM + 1] + psink))

        st_a, st_b, st_c, st_d = {}, {}, {}, {}
        for step in range(len(chains) + SWA_VALUE_LAG):
            if step < len(chains):
                st_a[step] = scores(*chains[step])
            if 0 <= step - 1 < len(chains):
                st_b[step - 1] = masked_max(*st_a.pop(step - 1))
            if 0 <= step - 2 < len(chains):
                st_c[step - 2] = probs(*st_b.pop(step - 2))
            if 0 <= step - SWA_VALUE_LAG < len(chains):
                i, g, v = chains[step - SWA_VALUE_LAG]
                st_d[v] = values(i, g, *st_c.pop(step - SWA_VALUE_LAG))
                if v == 1:
                    ot = jnp.concatenate([st_d.pop(0), st_d.pop(1)], axis=0)
                    for half, lb in enumerate((2 * g, 2 * g + 1)):
                        o_ref[pl.ds(q0[i], BLOCK), lb * LANES:(lb + 1) * LANES] = (
                            ot[:, half * LANES:(half + 1) * LANES].T.astype(BF16))
        return carry
    lax.fori_loop(0, SEQ // (BLOCK * SWA_UNROLL), body, 0)


def _swa(sink, proj3, proj_lead):
    b = proj3.shape[0]
    return pl.pallas_call(
        _swa_kernel,
        grid=(b,),
        in_specs=[
            pl.BlockSpec(memory_space=pltpu.SMEM),
            pl.BlockSpec((None, SEQ, 1024), lambda i: (i, 0, QA_OFF // 1024)),
            pl.BlockSpec((None, SEQ, 256), lambda i: (i, 0, KA_OFF // 256)),
            pl.BlockSpec((None, SEQ, 256), lambda i: (i, 0, VA_OFF // 256)),
            pl.BlockSpec((BLOCK, 256), lambda i: (0, KA_OFF // 256)),
            pl.BlockSpec((BLOCK, 256), lambda i: (0, VA_OFF // 256)),
        ],
        out_specs=pl.BlockSpec((None, SEQ, 1024), lambda i: (i, 0, 0)),
        out_shape=jax.ShapeDtypeStruct((b, SEQ, 1024), BF16),
        scratch_shapes=[
            pltpu.VMEM((2 * SWA_KV_HEADS, SEQ, LANES), BF16),
            pltpu.VMEM((2 * SWA_KV_HEADS, N_META, LANES), BF16),
            pltpu.VMEM((SWA_KV_HEADS, SWA_VROWS, KEYS_EXT), BF16),
        ],
        compiler_params=_cparams(1),
        name="swa_attn",
    )(sink, proj3, proj3, proj3, proj_lead, proj_lead)


DIFF_VROWS = 2 * HEAD_DIM + BF16_ROWS
DIFF_KCHUNK = 512
DIFF_PART = 512


def _diff_kernel(lamv_ref, gain_ref, q_ref, k_ref, v_ref, km_ref, vm_ref, o_ref, kx_ref, vt_ref,
                 s0_ref, s1_ref, e0_ref, e1_ref, *, lambda_init, tq):
    s_refs, e_refs = (s0_ref, s1_ref), (e0_ref, e1_ref)

    @pl.when(pl.program_id(2) == 0)
    def _():
        lo = lax.broadcasted_iota(jnp.int32, (1, LANES), 1) < HEAD_DIM
        zero = jnp.zeros((), BF16)

        def put_k(kk, sl):
            kx_ref[0, sl, :] = jnp.where(lo, kk, zero)
            kx_ref[1, sl, :] = jnp.where(lo, zero, kk)
        for c in range(SEQ // 512):
            put_k(k_ref[c * 512:(c + 1) * 512, :], slice(c * 512, (c + 1) * 512))
            for cc in range(2):
                sl = slice(c * 512 + cc * 256, c * 512 + (cc + 1) * 256)
                vt_ref[0:2 * HEAD_DIM, sl] = v_ref[sl, :].astype(F32).T.astype(BF16)
        put_k(km_ref[META_PAD:BLOCK, :], slice(SEQ, KEYS_VALID))
        vt_ref[0:2 * HEAD_DIM, SEQ:KEYS_EXT] = vm_ref[...].astype(F32).T.astype(BF16)
        vt_ref[2 * HEAD_DIM:DIFF_VROWS, :] = jnp.ones((BF16_ROWS, KEYS_EXT), BF16)
        for e_ref in e_refs:
            e_ref[SEQ:SEQ + META_PAD, :] = jnp.zeros((META_PAD, 2 * DIFF_PART), BF16)

    lv = lamv_ref[...]
    lam = (jnp.exp(jnp.sum(lv[0:1] * lv[1:2], axis=-1, keepdims=True))
           - jnp.exp(jnp.sum(lv[2:3] * lv[3:4], axis=-1, keepdims=True)) + lambda_init)
    chunks = [(k0, DIFF_KCHUNK if k0 + DIFF_KCHUNK < SEQ else KEYS_VALID - k0) for k0 in range(0, SEQ, DIFF_KCHUNK)]
    qrows = [slice(p * DIFF_PART, (p + 1) * DIFF_PART) for p in range(tq // DIFF_PART)]
    qcols = [slice((p % 2) * DIFF_PART, (p % 2 + 1) * DIFF_PART) for p in range(tq // DIFF_PART)]

    t = pl.program_id(1) * pl.num_programs(2) + pl.program_id(2)
    half_st = t & 1
    half_ld = t - ((t >> 1) << 1)

    acc, mx, out = {}, {}, {}

    def scores(it, k0, kn):
        qh, mp = it
        s = _nt_dot(kx_ref[mp, k0:k0 + kn, :], q_ref[qrows[qh], :])
        s_refs[mp][half_st, k0:k0 + kn, qcols[qh]] = s
        a = acc.setdefault(it, [None, None])
        for i, r in enumerate(range(0, kn, F32_ROWS)):
            x = s[r:r + F32_ROWS, :]
            a[i % 2] = x if a[i % 2] is None else jnp.maximum(a[i % 2], x)

    def col_max(it):
        m8 = jnp.maximum(*acc.pop(it))
        mx[it] = jnp.broadcast_to(jnp.max(m8, axis=0, keepdims=True), (BF16_ROWS, m8.shape[1]))

    def exps(it, k0, kn):
        qh, mp = it
        for r in range(k0, k0 + kn, BF16_ROWS):
            x = s_refs[mp][half_ld, r:r + BF16_ROWS, qcols[qh]]
            er = r if r < SEQ else r + META_PAD
            e_refs[mp][er:er + BF16_ROWS, qcols[qh]] = jnp.exp2((x - mx[it]).astype(BF16))

    def values(it):
        qh, mp = it
        out[it] = jnp.dot(vt_ref[...], e_refs[mp][:, qcols[qh]], preferred_element_type=F32)

    def finish(qh):
        d = 2 * HEAD_DIM
        o1, o2 = out.pop((qh, 0)), out.pop((qh, 1))
        r1 = 1.0 / o1[d:d + 1]
        r2 = lam / o2[d:d + 1]
        o = (o1[0:d] * r1 - o2[0:d] * r2).T
        y = o * lax.rsqrt(jnp.mean(o * o, axis=-1, keepdims=True) + RMS_EPS) * gain_ref[...]
        o_ref[qrows[qh], :] = (y * (1.0 - lambda_init)).astype(BF16)

    def staged(score_it=None, exp_it=None):
        if exp_it is not None:
            col_max(exp_it)
        for ch in chunks:
            if score_it is not None:
                scores(score_it, *ch)
            if exp_it is not None:
                exps(exp_it, *ch)

    items = [(p, mp) for p in range(len(qrows)) for mp in range(2)]
    for n in range(len(items) + 1):
        staged(score_it=items[n] if n < len(items) else None, exp_it=items[n - 1] if n > 0 else None)
        if n > 0:
            values(items[n - 1])
            if items[n - 1][1] == 1:
                finish(items[n - 1][0])


def _diff(lamv, gain, proj3, proj_lead, *, lambda_init, tq):
    b = proj3.shape[0]
    qb, kb, vb = QB_OFF // LANES, KB_OFF // LANES, VB_OFF // LANES
    kern = functools.partial(_diff_kernel, lambda_init=lambda_init, tq=tq)
    return pl.pallas_call(
        kern,
        grid=(b, DIFF_HEADS, SEQ // tq),
        in_specs=[
            pl.BlockSpec((4, HEAD_DIM), lambda i, h, t: (0, 0)),
            pl.BlockSpec((1, LANES), lambda i, h, t: (0, 0)),
            pl.BlockSpec((None, tq, LANES), lambda i, h, t: (i, t, qb + h)),
            pl.BlockSpec((None, SEQ, LANES), lambda i, h, t: (i, 0, kb + h)),
            pl.BlockSpec((None, SEQ, LANES), lambda i, h, t: (i, 0, vb + h)),
            pl.BlockSpec((BLOCK, LANES), lambda i, h, t: (0, kb + h)),
            pl.BlockSpec((BLOCK, LANES), lambda i, h, t: (0, vb + h)),
        ],
        out_specs=pl.BlockSpec((None, tq, LANES), lambda i, h, t: (i, t, h)),
        out_shape=jax.ShapeDtypeStruct((b, SEQ, DIFF_HEADS * LANES), BF16),
        scratch_shapes=[
            pltpu.VMEM((2, KEYS_VALID, LANES), BF16),
            pltpu.VMEM((DIFF_VROWS, KEYS_EXT), BF16),
            pltpu.VMEM((2, KEYS_VALID, 2 * DIFF_PART), F32),
            pltpu.VMEM((2, KEYS_VALID, 2 * DIFF_PART), F32),
            pltpu.VMEM((KEYS_EXT, 2 * DIFF_PART), BF16),
            pltpu.VMEM((KEYS_EXT, 2 * DIFF_PART), BF16),
        ],
        compiler_params=_cparams(3),
        name="diff_attn",
    )(lamv, gain, proj3, proj3, proj3, proj_lead, proj_lead)


def _merge_kernel(x_ref, os_ref, od_ref, gs_ref, gd_ref, ws_ref, wd_ref, wo_ref, gain_ref, o_ref):
    for r0 in range(0, x_ref.shape[0], ROWS_PER_PASS):
        rows = slice(r0, r0 + ROWS_PER_PASS)
        a = jnp.dot(os_ref[rows, :], ws_ref[...], preferred_element_type=F32)
        b = jnp.dot(od_ref[rows, :], wd_ref[...], preferred_element_type=F32)
        merged = gs_ref[rows, :].astype(F32) * a + gd_ref[rows, :].astype(F32) * b
        y = jnp.dot(merged.astype(BF16), wo_ref[...], preferred_element_type=F32)
        o_ref[rows, :] = (x_ref[rows, :]
                          + (y * lax.rsqrt(jnp.mean(y * y, axis=-1, keepdims=True) + RMS_EPS)) * gain_ref[...])


def _merge(x2d, o_swa, o_diff, proj, ws, wd, wo, gain, *, tm):
    n = x2d.shape[0]
    wspec = _resident((D_MODEL, D_MODEL))
    return pl.pallas_call(
        _merge_kernel,
        grid=(n // tm,),
        in_specs=[
            pl.BlockSpec((tm, D_MODEL), lambda i: (i, 0)),
            pl.BlockSpec((tm, D_MODEL), lambda i: (i, 0)),
            pl.BlockSpec((tm, D_MODEL), lambda i: (i, 0)),
            pl.BlockSpec((tm, D_MODEL), lambda i: (i, G_OFF // D_MODEL)),
            pl.BlockSpec((tm, D_MODEL), lambda i: (i, G_OFF // D_MODEL + 1)),
            wspec, wspec, wspec,
            _resident((1, D_MODEL)),
        ],
        out_specs=pl.BlockSpec((tm, D_MODEL), lambda i: (i, 0)),
        out_shape=jax.ShapeDtypeStruct((n, D_MODEL), F32),
        compiler_params=_cparams(1),
        name="merge_out",
    )(x2d, o_swa, o_diff, proj, proj, ws, wd, wo, gain)


FF_CHUNK = 256


def _ffn_kernel(h_ref, g1_ref, wi_ref, wo_ref, g2_ref, o_ref, f_ref):
    for r0 in range(0, h_ref.shape[0], ROWS_PER_PASS):
        rows = slice(r0, r0 + ROWS_PER_PASS)
        h = h_ref[rows, :]
        u = ((h * lax.rsqrt(jnp.mean(h * h, axis=-1, keepdims=True) + RMS_EPS)) * g1_ref[...]).astype(BF16)
        for c in range(D_FF // FF_CHUNK):
            gate = jnp.dot(u, wi_ref[:, c * FF_CHUNK:(c + 1) * FF_CHUNK], preferred_element_type=F32)
            up = jnp.dot(u, wi_ref[:, D_FF + c * FF_CHUNK:D_FF + (c + 1) * FF_CHUNK], preferred_element_type=F32)
            f_ref[rows, c * FF_CHUNK:(c + 1) * FF_CHUNK] = (jax.nn.silu(gate) * up).astype(BF16)
        f = jnp.dot(f_ref[rows, :], wo_ref[...], preferred_element_type=F32)
        o_ref[rows, :] = h + (f * lax.rsqrt(jnp.mean(f * f, axis=-1, keepdims=True) + RMS_EPS)) * g2_ref[...]


def _ffn(h2d, g1, wi, wo, g2, *, tm):
    n = h2d.shape[0]
    return pl.pallas_call(
        _ffn_kernel,
        grid=(n // tm,),
        in_specs=[
            pl.BlockSpec((tm, D_MODEL), lambda i: (i, 0)),
            _resident((1, D_MODEL)),
            _resident((D_MODEL, 2 * D_FF)),
            _resident((D_FF, D_MODEL)),
            _resident((1, D_MODEL)),
        ],
        out_specs=pl.BlockSpec((tm, D_MODEL), lambda i: (i, 0)),
        out_shape=jax.ShapeDtypeStruct((n, D_MODEL), F32),
        scratch_shapes=[pltpu.VMEM((tm, D_FF), BF16)],
        compiler_params=_cparams(1),
        name="ffn",
    )(h2d, g1, wi, wo, g2)


def _permute_cols(a):
    qa, ka, va, rest = a[..., :1024], a[..., 1024:1280], a[..., 1280:1536], a[..., 1536:]
    return jnp.concatenate([qa, rest, ka, va], axis=-1)


def kernel(x, meta_tokens, pre_mix_gain, w_in, b_gate, attn_sink, lambda_q1, lambda_k1, lambda_q2, lambda_k2,
           diff_subln_gain, w_branch_swa, w_branch_diff, w_out, post_mix_gain, pre_ffn_gain, w_ffn_in, w_ffn_out,
           post_ffn_gain):
    bsz = x.shape[0]
    if w_in.shape[0] != 1:
        raise NotImplementedError("DEPTH > 1 is not supported")
    l = 0
    lambda_init = 0.8 - 0.6 * math.exp(-0.3 * l)
    row = lambda a: a.reshape(1, -1).astype(F32)

    x2d = x.reshape(bsz * SEQ, D_MODEL)
    lead = jnp.concatenate([jnp.zeros((META_PAD, D_MODEL), x.dtype), meta_tokens.astype(x.dtype)], axis=0)
    real_tables = _rope_tables(jnp.arange(SEQ, dtype=jnp.int32) + N_META)
    lead_tables = _rope_tables(jnp.arange(BLOCK, dtype=jnp.int32) - META_PAD)

    w = _permute_cols(w_in[l]).astype(BF16)
    proj = _inproj(x2d, row(pre_mix_gain[l]), w, row(b_gate[l]), *real_tables, tm=TOKENS_PER_STEP)
    proj_lead = _inproj(lead, row(pre_mix_gain[l]), w, row(b_gate[l]), *lead_tables, tm=BLOCK)
    proj3 = proj.reshape(bsz, SEQ, IN_COLS)

    o_swa = _swa(row(attn_sink[l]), proj3, proj_lead)
    lamv = jnp.stack([lambda_q1[l], lambda_k1[l], lambda_q2[l], lambda_k2[l]]).astype(F32)
    o_diff = _diff(lamv, row(diff_subln_gain[l]), proj3, proj_lead, lambda_init=lambda_init,
                   tq=DIFF_QUERIES_PER_STEP)

    h2d = _merge(x2d, o_swa.reshape(bsz * SEQ, D_MODEL), o_diff.reshape(bsz * SEQ, D_MODEL), proj,
                 w_branch_swa[l].astype(BF16), w_branch_diff[l].astype(BF16), w_out[l].astype(BF16),
                 row(post_mix_gain[l]), tm=TOKENS_PER_STEP)
    h2d = _ffn(h2d, row(pre_ffn_gain[l]), w_ffn_in[l].astype(BF16), w_ffn_out[l].astype(BF16),
               row(post_ffn_gain[l]), tm=TOKENS_PER_STEP)
    return h2d.reshape(bsz, SEQ, D_MODEL)
```
